```python
import math
import jax, jax.numpy as jnp
from jax import lax
import numpy as np

D_MODEL = 2048
BATCH = 1
SEQ = 8192
DEPTH = 4
DEC_BATCH = 8
DEC_SEQ = 32
PAST_LEN = 1024

CHUNK = 64
A_HEADS = 8
A_HEAD_DIM = 128
A_WIDTH = A_HEADS * A_HEAD_DIM
A_CHUNK = 128
N_HEADS = 16
N_KV_HEADS = 2
HEAD_DIM = 64
Q_WIDTH = N_HEADS * HEAD_DIM
KV_WIDTH = N_KV_HEADS * HEAD_DIM
WINDOW = 128
WINDOW_CHUNKS = WINDOW // CHUNK
EVEN_IN = 2 * A_WIDTH + Q_WIDTH + 2 * KV_WIDTH
EVEN_MIX = A_WIDTH + Q_WIDTH
CONV_WIDTH = 3
N_EXPERTS = 16
N_GROUPS = 4
EXPERTS_PER_GROUP = N_EXPERTS // N_GROUPS
TOP_K = 2
D_EXPERT = 1024
ALPHA = (2 * DEPTH) ** 0.25
BETA = (8 * DEPTH) ** -0.25
N_EVEN = (DEPTH + 1) // 2
N_ODD = DEPTH // 2
LN_EPS = 1e-5
NEG_INF = -1e30

kernel_name = "hybrid_chunk_stream_encoder_step"


def layer_norm(x, g, b):
    xf = x.astype(jnp.float32)
    mu = jnp.mean(xf, axis=-1, keepdims=True)
    var = jnp.mean(jnp.square(xf - mu), axis=-1, keepdims=True)
    y = (xf - mu) * lax.rsqrt(var + LN_EPS) * g.astype(jnp.float32) + b.astype(jnp.float32)
    return y.astype(x.dtype)


def alibi_slopes():
    return jnp.exp2(-8.0 * jnp.arange(1, N_HEADS + 1, dtype=jnp.float32) / N_HEADS)


def sink_softmax(scores, sinks):
    sink = sinks.astype(jnp.float32)[:, :, None, None]
    m = jnp.maximum(jnp.max(scores, axis=-1, keepdims=True), sink)
    p = jnp.exp(scores - m)
    return p / (jnp.sum(p, axis=-1, keepdims=True) + jnp.exp(sink - m))


def split_even(h):
    return jnp.split(h, [A_WIDTH, 2 * A_WIDTH, 2 * A_WIDTH + Q_WIDTH,
                         2 * A_WIDTH + Q_WIDTH + KV_WIDTH], axis=-1)


def chunk_mlp(u, v, ln_g, ln_b, w_s, b_s):
    bsz, L, _ = v.shape
    blk = min(L, A_CHUNK)
    n = L // blk
    vn = layer_norm(v, ln_g, ln_b)
    w = jnp.tril(w_s[:, :blk, :blk])
    vb = vn.reshape(bsz, n, blk, A_HEADS, A_HEAD_DIM)
    s = jnp.einsum('hij,bnjhd->bnihd', w, vb) + jnp.swapaxes(b_s[:, :blk], 0, 1)[:, :, None]
    return u * s.reshape(bsz, L, A_WIDTH), vn


def window_attention_prompt(q, k, v, sinks):
    bsz, S = q.shape[:2]
    n = S // CHUNK
    grp = N_HEADS // N_KV_HEADS
    nb = WINDOW_CHUNKS + 1
    qb = q.reshape(bsz, n, CHUNK, N_KV_HEADS, grp, HEAD_DIM)
    pad = ((0, 0), (WINDOW_CHUNKS, 0), (0, 0), (0, 0), (0, 0))
    kp = jnp.pad(k.reshape(bsz, n, CHUNK, N_KV_HEADS, HEAD_DIM), pad)
    vp = jnp.pad(v.reshape(bsz, n, CHUNK, N_KV_HEADS, HEAD_DIM), pad)
    kb = jnp.concatenate([kp[:, j:j + n] for j in range(nb)], axis=2)
    vb = jnp.concatenate([vp[:, j:j + n] for j in range(nb)], axis=2)
    scores = jnp.einsum('bnqkgd,bnskd->bnkgqs', qb, kb,
                        preferred_element_type=jnp.float32) * (HEAD_DIM ** -0.5)
    cidx = jnp.arange(n)
    qpos = cidx[:, None] * CHUNK + jnp.arange(CHUNK)[None]
    kpos = (cidx[:, None] - WINDOW_CHUNKS) * CHUNK + jnp.arange(nb * CHUNK)[None]
    dist = jnp.abs(qpos[:, :, None] - kpos[:, None, :]).astype(jnp.float32)
    slopes = alibi_slopes().reshape(N_KV_HEADS, grp)
    bias = -slopes[None, :, :, None, None] * dist[:, None, None]
    scores = jnp.where((kpos >= 0)[:, None, None, None, :], scores + bias, NEG_INF)
    probs = sink_softmax(scores, sinks.reshape(N_KV_HEADS, grp))
    out = jnp.einsum('bnkgqs,bnskd->bnqkgd', probs.astype(v.dtype), vb)
    return out.reshape(bsz, S, Q_WIDTH)


def window_attention_sample(q, k_all, v_all, sinks):
    bsz, L = q.shape[:2]
    S = k_all.shape[1]
    w_c = S - L
    grp = N_HEADS // N_KV_HEADS
    qg = q.reshape(bsz, L, N_KV_HEADS, grp, HEAD_DIM)
    scores = jnp.einsum('bqkgd,bskd->bkgqs', qg, k_all,
                        preferred_element_type=jnp.float32) * (HEAD_DIM ** -0.5)
    dist = jnp.abs((w_c + jnp.arange(L))[:, None] - jnp.arange(S)[None]).astype(jnp.float32)
    slopes = alibi_slopes().reshape(N_KV_HEADS, grp)
    scores = scores - slopes[:, :, None, None] * dist
    probs = sink_softmax(scores, sinks.reshape(N_KV_HEADS, grp))
    out = jnp.einsum('bkgqs,bskd->bqkgd', probs.astype(v_all.dtype), v_all)
    return out.reshape(bsz, L, Q_WIDTH)


def even_mixer_prompt(x, w_in, v_g, v_b, w_s, b_s, sinks, w_o):
    bsz, S, _ = x.shape
    u, v, q, k, vv = split_even(x @ w_in)
    a_out, _ = chunk_mlp(u, v, v_g, v_b, w_s, b_s)
    k = k.reshape(bsz, S, N_KV_HEADS, HEAD_DIM)
    vv = vv.reshape(bsz, S, N_KV_HEADS, HEAD_DIM)
    b_out = window_attention_prompt(q.reshape(bsz, S, N_HEADS, HEAD_DIM), k, vv, sinks)
    y = jnp.concatenate([a_out, b_out], axis=-1) @ w_o
    n_keep = min(WINDOW, S)
    return y, k[:, S - n_keep:], vv[:, S - n_keep:]


def even_mixer_sample(x, cache_k, cache_v, w_in, v_g, v_b, w_s, b_s, sinks, w_o):
    bsz, L, _ = x.shape
    u, v, q, k, vv = split_even(x @ w_in)
    a_out, vn = chunk_mlp(u, v, v_g, v_b, w_s, b_s)
    k = k.reshape(bsz, L, N_KV_HEADS, HEAD_DIM)
    vv = vv.reshape(bsz, L, N_KV_HEADS, HEAD_DIM)
    k_all = jnp.concatenate([cache_k, k], axis=1)
    v_all = jnp.concatenate([cache_v, vv], axis=1)
    b_out = window_attention_sample(q.reshape(bsz, L, N_HEADS, HEAD_DIM), k_all, v_all, sinks)
    y = jnp.concatenate([a_out, b_out], axis=-1) @ w_o
    return y, k, vv, vn


def conv_mixer(x, left, w_in, conv_w, w_out):
    L = x.shape[1]
    gate_out, gate_in, h = jnp.split(x @ w_in, 3, axis=-1)
    z = gate_in * h
    zp = jnp.concatenate([left, z], axis=1)
    conv = conv_w[0] * zp[:, 0:L]
    for j in range(1, CONV_WIDTH):
        conv = conv + conv_w[j] * zp[:, j:j + L]
    return (gate_out * conv) @ w_out, zp[:, L:]


def grouped_moe(x, w_router, b_router, w_gate, w_up, w_down):
    shp = x.shape
    xt = x.reshape(-1, D_MODEL)
    n_tok = xt.shape[0]
    scores = jax.nn.softmax((xt @ w_router).astype(jnp.float32), axis=-1)
    biased = (scores + b_router.astype(jnp.float32)).reshape(n_tok, N_GROUPS, EXPERTS_PER_GROUP)
    group_score = jnp.sum(lax.top_k(biased, TOP_K)[0], axis=-1)
    grp = jnp.argmax(group_score, axis=-1)
    in_group = biased[jnp.arange(n_tok), grp]
    _, local = lax.top_k(in_group, TOP_K)
    expert = grp[:, None] * EXPERTS_PER_GROUP + local
    gate = jnp.take_along_axis(scores, expert, axis=-1)
    gate = gate / jnp.sum(gate, axis=-1, keepdims=True)
    combine = jnp.sum(jax.nn.one_hot(expert, N_EXPERTS, dtype=jnp.float32) * gate[..., None],
                      axis=1).astype(x.dtype)
    y = jnp.zeros_like(xt)
    for e in range(N_EXPERTS):
        h = jax.nn.silu(xt @ w_gate[e]) * (xt @ w_up[e])
        y = y + combine[:, e:e + 1] * (h @ w_down[e])
    return y.reshape(shp)


def setup_inputs(seed: int = 0) -> dict:
    key = jax.random.key(seed)
    ks = jax.random.split(key, 24)
    f32 = jnp.float32

    def nrm(k, shape, scale=1.0):
        return jax.random.normal(k, shape, f32) * scale

    win = min(WINDOW, PAST_LEN)
    return {
        "x_prompt": nrm(ks[0], (BATCH, SEQ, D_MODEL)),
        "x_sample": nrm(ks[1], (DEC_BATCH, DEC_SEQ, D_MODEL)),
        "cache_k": nrm(ks[2], (N_EVEN, DEC_BATCH, win, N_KV_HEADS, HEAD_DIM)),
        "cache_v": nrm(ks[3], (N_EVEN, DEC_BATCH, win, N_KV_HEADS, HEAD_DIM)),
        "state_conv": nrm(ks[4], (N_ODD, DEC_BATCH, CONV_WIDTH - 1, D_MODEL)),
        "w_in_even": nrm(ks[5], (N_EVEN, D_MODEL, EVEN_IN), D_MODEL ** -0.5),
        "ln_v_g": 1.0 + nrm(ks[6], (N_EVEN, A_WIDTH), 0.02),
        "ln_v_b": nrm(ks[7], (N_EVEN, A_WIDTH), 0.02),
        "w_spatial": nrm(ks[8], (N_EVEN, A_HEADS, A_CHUNK, A_CHUNK), A_CHUNK ** -0.5),
        "b_spatial": 1.0 + nrm(ks[9], (N_EVEN, A_HEADS, A_CHUNK), 0.1),
        "sinks": nrm(ks[10], (N_EVEN, N_HEADS)),
        "w_out_even": nrm(ks[11], (N_EVEN, EVEN_MIX, D_MODEL), BETA * EVEN_MIX ** -0.5),
        "w_in_odd": nrm(ks[12], (N_ODD, D_MODEL, 3 * D_MODEL), D_MODEL ** -0.5),
        "conv_w": nrm(ks[13], (N_ODD, CONV_WIDTH, D_MODEL), CONV_WIDTH ** -0.5),
        "w_out_odd": nrm(ks[14], (N_ODD, D_MODEL, D_MODEL), BETA * D_MODEL ** -0.5),
        "ln_mix_g": 1.0 + nrm(ks[15], (DEPTH, D_MODEL), 0.02),
        "ln_mix_b": nrm(ks[16], (DEPTH, D_MODEL), 0.02),
        "ln_ffn_g": 1.0 + nrm(ks[17], (DEPTH, D_MODEL), 0.02),
        "ln_ffn_b": nrm(ks[18], (DEPTH, D_MODEL), 0.02),
        "w_router": nrm(ks[19], (D_MODEL, N_EXPERTS), D_MODEL ** -0.5),
        "b_router": nrm(ks[20], (N_EXPERTS,), 0.01),
        "w_gate": nrm(ks[21], (DEPTH, N_EXPERTS, D_MODEL, D_EXPERT), D_MODEL ** -0.5),
        "w_up": nrm(ks[22], (DEPTH, N_EXPERTS, D_MODEL, D_EXPERT), D_MODEL ** -0.5),
        "w_down": nrm(ks[23], (DEPTH, N_EXPERTS, D_EXPERT, D_MODEL), BETA * D_EXPERT ** -0.5),
    }


def reference(x_prompt, x_sample, cache_k, cache_v, state_conv, w_in_even, ln_v_g, ln_v_b,
              w_spatial, b_spatial, sinks, w_out_even, w_in_odd, conv_w, w_out_odd,
              ln_mix_g, ln_mix_b, ln_ffn_g, ln_ffn_b, w_router, b_router, w_gate, w_up, w_down):
    xp, xs = x_prompt, x_sample
    k_p, v_p, k_s, v_s, vg_s, c_p, c_s = [], [], [], [], [], [], []
    for layer in range(DEPTH):
        i = layer // 2
        if layer % 2 == 0:
            mp, kpn, vpn = even_mixer_prompt(xp, w_in_even[i], ln_v_g[i], ln_v_b[i], w_spatial[i],
                                             b_spatial[i], sinks[i], w_out_even[i])
            ms, ksn, vsn, vgn = even_mixer_sample(xs, cache_k[i], cache_v[i], w_in_even[i],
                                                  ln_v_g[i], ln_v_b[i], w_spatial[i],
                                                  b_spatial[i], sinks[i], w_out_even[i])
            k_p.append(kpn)
            v_p.append(vpn)
            k_s.append(ksn)
            v_s.append(vsn)
            vg_s.append(vgn)
        else:
            zeros_left = jnp.zeros((xp.shape[0], CONV_WIDTH - 1, D_MODEL), xp.dtype)
            mp, cpn = conv_mixer(xp, zeros_left, w_in_odd[i], conv_w[i], w_out_odd[i])
            ms, csn = conv_mixer(xs, state_conv[i], w_in_odd[i], conv_w[i], w_out_odd[i])
            c_p.append(cpn)
            c_s.append(csn)
        xp = layer_norm(ALPHA * xp + mp, ln_mix_g[layer], ln_mix_b[layer])
        xs = layer_norm(ALPHA * xs + ms, ln_mix_g[layer], ln_mix_b[layer])
        xp = layer_norm(ALPHA * xp + grouped_moe(xp, w_router, b_router, w_gate[layer],
                                                 w_up[layer], w_down[layer]),
                        ln_ffn_g[layer], ln_ffn_b[layer])
        xs = layer_norm(ALPHA * xs + grouped_moe(xs, w_router, b_router, w_gate[layer],
                                                 w_up[layer], w_down[layer]),
                        ln_ffn_g[layer], ln_ffn_b[layer])
    return (xp, xs, jnp.stack(k_p), jnp.stack(v_p), jnp.stack(k_s), jnp.stack(v_s),
            jnp.stack(vg_s), jnp.stack(c_p), jnp.stack(c_s))
```

```python
import functools

import jax
import jax.numpy as jnp
from jax import lax
from jax.experimental import pallas as pl
from jax.experimental.pallas import tpu as pltpu

F32 = jnp.float32
BF16 = jnp.bfloat16
I32 = jnp.int32

D_MODEL = 2048
DEPTH = 4
CHUNK = 64
A_HEADS = 8
A_HEAD_DIM = 128
A_WIDTH = A_HEADS * A_HEAD_DIM
A_CHUNK = 128
N_HEADS = 16
N_KV_HEADS = 2
HEAD_DIM = 64
Q_WIDTH = N_HEADS * HEAD_DIM
KV_WIDTH = N_KV_HEADS * HEAD_DIM
WINDOW = 128
CONV_WIDTH = 3
N_EXPERTS = 16
N_GROUPS = 4
EXPERTS_PER_GROUP = N_EXPERTS // N_GROUPS
D_EXPERT = 1024
ALPHA = (2 * DEPTH) ** 0.25
LN_EPS = 1e-5
NEG_INF = -1e30
ALIBI_SLOPES = tuple(2.0 ** (-8.0 * h / N_HEADS) for h in range(1, N_HEADS + 1))

LANES = 128
ROW_BLK = 256
MM_ROWS = 768
MOE_ROWS = 512
MOE_FCHUNK = 256
VMEM_LIMIT = 56 * 1024 * 1024


def _params(n_axes, **kw):
    return pltpu.CompilerParams(dimension_semantics=("arbitrary",) * n_axes,
                                vmem_limit_bytes=VMEM_LIMIT, **kw)


def _layer_norm(y, g, b):
    mu = jnp.mean(y, axis=-1, keepdims=True)
    yc = y - mu
    var = jnp.mean(yc * yc, axis=-1, keepdims=True)
    return yc * lax.rsqrt(var + LN_EPS) * g + b


def _matmul_cols_kernel(x_ref, w_ref, o_ref, wb_ref):
    @pl.when(pl.program_id(1) == 0)
    def _():
        wb_ref[...] = w_ref[0].astype(BF16)

    o_ref[...] = jnp.dot(x_ref[...], wb_ref[...], preferred_element_type=F32).astype(o_ref.dtype)


def _matmul_cols(xb, w, li, col_blk0, n_tiles, tn, out_dtype):
    t, k = xb.shape
    return pl.pallas_call(
        _matmul_cols_kernel,
        grid=(n_tiles, t // MM_ROWS),
        in_specs=[pl.BlockSpec((MM_ROWS, k), lambda j, i: (i, 0)),
                  pl.BlockSpec((1, k, tn), lambda j, i: (li, 0, col_blk0 + j))],
        out_specs=pl.BlockSpec((MM_ROWS, tn), lambda j, i: (i, j)),
        out_shape=jax.ShapeDtypeStruct((t, n_tiles * tn), out_dtype),
        scratch_shapes=[pltpu.VMEM((k, tn), BF16)],
        compiler_params=_params(2),
        name="matmul_cols",
    )(xb, w)


def _odd_in_kernel(x_ref, wo_ref, wi_ref, wh_ref, go_ref, z_ref, wb_ref):
    @pl.when(pl.program_id(1) == 0)
    def _():
        wb_ref[0] = wo_ref[0].astype(BF16)
        wb_ref[1] = wi_ref[0].astype(BF16)
        wb_ref[2] = wh_ref[0].astype(BF16)

    x = x_ref[...]
    go_ref[...] = jnp.dot(x, wb_ref[0], preferred_element_type=F32)
    gate_in = jnp.dot(x, wb_ref[1], preferred_element_type=F32)
    h = jnp.dot(x, wb_ref[2], preferred_element_type=F32)
    z_ref[...] = gate_in * h


def _odd_in_proj(xb, w_in, li):
    t, k = xb.shape
    tn = 256
    nt = D_MODEL // tn
    out = jax.ShapeDtypeStruct((t, D_MODEL), F32)
    return pl.pallas_call(
        _odd_in_kernel,
        grid=(nt, t // MM_ROWS),
        in_specs=[pl.BlockSpec((MM_ROWS, k), lambda j, i: (i, 0)),
                  pl.BlockSpec((1, k, tn), lambda j, i: (li, 0, j)),
                  pl.BlockSpec((1, k, tn), lambda j, i: (li, 0, nt + j)),
                  pl.BlockSpec((1, k, tn), lambda j, i: (li, 0, 2 * nt + j))],
        out_specs=[pl.BlockSpec((MM_ROWS, tn), lambda j, i: (i, j)),
                   pl.BlockSpec((MM_ROWS, tn), lambda j, i: (i, j))],
        out_shape=[out, out],
        scratch_shapes=[pltpu.VMEM((3, k, tn), BF16)],
        compiler_params=_params(2),
        name="odd_in_proj",
    )(xb, w_in, w_in, w_in)


def _out_proj_kernel(nk, l_ref, w_ref, r_ref, g_ref, b_ref, of_ref, ob_ref):
    k = pl.program_id(1)
    part = jnp.dot(l_ref[...], w_ref[0].astype(BF16), preferred_element_type=F32)

    @pl.when(k == 0)
    def _():
        of_ref[...] = ALPHA * r_ref[...] + part

    @pl.when(k > 0)
    def _():
        of_ref[...] += part

    @pl.when(k == nk - 1)
    def _():
        y = _layer_norm(of_ref[...], g_ref[...], b_ref[...])
        of_ref[...] = y
        ob_ref[...] = y.astype(BF16)


def _out_proj_ln(lhs, w, li, resid, g, b):
    t, kdim = lhs.shape
    tk = 512
    nk = kdim // tk
    row = lambda i, k: (i, 0)
    return pl.pallas_call(
        functools.partial(_out_proj_kernel, nk),
        grid=(t // MM_ROWS, nk),
        in_specs=[pl.BlockSpec((MM_ROWS, tk), lambda i, k: (i, k)),
                  pl.BlockSpec((1, tk, D_MODEL), lambda i, k: (li, k, 0)),
                  pl.BlockSpec((MM_ROWS, D_MODEL), row),
                  pl.BlockSpec((1, D_MODEL), lambda i, k: (0, 0)),
                  pl.BlockSpec((1, D_MODEL), lambda i, k: (0, 0))],
        out_specs=[pl.BlockSpec((MM_ROWS, D_MODEL), row),
                   pl.BlockSpec((MM_ROWS, D_MODEL), row)],
        out_shape=[jax.ShapeDtypeStruct((t, D_MODEL), F32),
                   jax.ShapeDtypeStruct((t, D_MODEL), BF16)],
        compiler_params=_params(2),
        name="out_proj_ln",
    )(lhs, w, resid, g.reshape(1, -1), b.reshape(1, -1))


def _mixer_a_kernel(n_prompt_blk, seq_s, u_ref, v_ref, w_ref, bt_ref, g_ref, b_ref, a_ref, vn_ref):
    i = pl.program_id(0)
    is_sample = i == n_prompt_blk
    vn = _layer_norm(v_ref[...], g_ref[...], b_ref[...])

    @pl.when(is_sample)
    def _():
        vn_ref[...] = vn

    shift = jnp.where(is_sample, seq_s.bit_length() - 1, A_CHUNK.bit_length() - 1)
    low = jnp.where(is_sample, seq_s - 1, A_CHUNK - 1)
    r = lax.broadcasted_iota(I32, (A_CHUNK, A_CHUNK), 0)
    c = lax.broadcasted_iota(I32, (A_CHUNK, A_CHUNK), 1)
    mask = ((r >> shift) == (c >> shift)) & ((c & low) <= (r & low))
    vb = vn.astype(BF16)
    bt = bt_ref[0]
    for h in range(A_HEADS):
        wm = jnp.where(mask, w_ref[0, h], 0.0).astype(BF16)
        bcol = bt[:, h:h + 1]
        cs = slice(h * A_HEAD_DIM, (h + 1) * A_HEAD_DIM)
        for n in range(ROW_BLK // A_CHUNK):
            rs = slice(n * A_CHUNK, (n + 1) * A_CHUNK)
            s = jnp.dot(wm, vb[rs, cs], preferred_element_type=F32) + bcol
            a_ref[rs, cs] = (u_ref[rs, cs] * s).astype(BF16)


def _mixer_a(h3, w_stack, bt_stack, ln_g, ln_b, n_prompt_blk, seq_s):
    t = h3.shape[0]
    return pl.pallas_call(
        functools.partial(_mixer_a_kernel, n_prompt_blk, seq_s),
        grid=(t // ROW_BLK,),
        in_specs=[pl.BlockSpec((ROW_BLK, A_WIDTH), lambda i: (i, 0)),
                  pl.BlockSpec((ROW_BLK, A_WIDTH), lambda i: (i, 1)),
                  pl.BlockSpec((1, A_HEADS, A_CHUNK, A_CHUNK), lambda i: (i // n_prompt_blk, 0, 0, 0)),
                  pl.BlockSpec((1, A_CHUNK, A_HEADS), lambda i: (i // n_prompt_blk, 0, 0)),
                  pl.BlockSpec((1, A_WIDTH), lambda i: (0, 0)),
                  pl.BlockSpec((1, A_WIDTH), lambda i: (0, 0))],
        out_specs=[pl.BlockSpec((ROW_BLK, A_WIDTH), lambda i: (i, 0)),
                   pl.BlockSpec((ROW_BLK, A_WIDTH), lambda i: (0, 0))],
        out_shape=[jax.ShapeDtypeStruct((t, 2 * A_WIDTH), BF16),
                   jax.ShapeDtypeStruct((ROW_BLK, A_WIDTH), F32)],
        compiler_params=_params(1),
        name="mixer_a",
    )(h3, h3, w_stack, bt_stack, ln_g.reshape(1, -1), ln_b.reshape(1, -1))


def _attention_heads(q, k, v, dist, mask, sinks_ref, o_ref):
    grp = N_HEADS // N_KV_HEADS
    for kv in range(N_KV_HEADS):
        kg = k[:, kv * HEAD_DIM:(kv + 1) * HEAD_DIM]
        vg = v[:, kv * HEAD_DIM:(kv + 1) * HEAD_DIM]
        outs = []
        for j in range(grp):
            h = kv * grp + j
            qh = q[:, h * HEAD_DIM:(h + 1) * HEAD_DIM]
            s = lax.dot_general(qh, kg, (((1,), (1,)), ((), ())), preferred_element_type=F32)
            s = jnp.where(mask, s * (HEAD_DIM ** -0.5) - ALIBI_SLOPES[h] * dist, NEG_INF)
            sink = sinks_ref[h]
            m = jnp.maximum(jnp.max(s, axis=-1, keepdims=True), sink)
            p = jnp.exp(s - m)
            den = jnp.sum(p, axis=-1, keepdims=True) + jnp.exp(sink - m)
            outs.append(jnp.dot(p.astype(BF16), vg, preferred_element_type=F32) / den)
        for j in range(0, grp, 2):
            c0 = (kv * grp + j) * HEAD_DIM
            o_ref[:, c0:c0 + 2 * HEAD_DIM] = jnp.concatenate(outs[j:j + 2], axis=1).astype(BF16)


def _attn_prompt_kernel(sinks_ref, q_ref, kv_ref, halo_ref, mix_ref, o_ref):
    del mix_ref
    i = pl.program_id(0)
    kv_all = jnp.concatenate([halo_ref[...], kv_ref[...]], axis=0).astype(BF16)
    n_keys = WINDOW + ROW_BLK
    r = lax.broadcasted_iota(I32, (ROW_BLK, n_keys), 0)
    c = lax.broadcasted_iota(I32, (ROW_BLK, n_keys), 1)
    dist = jnp.abs(r + WINDOW - c).astype(F32)
    qc = r // CHUNK
    kc = c // CHUNK
    mask = (kc >= qc) & (kc <= qc + WINDOW // CHUNK) & (c >= jnp.where(i > 0, 0, WINDOW))
    _attention_heads(q_ref[...].astype(BF16), kv_all[:, :KV_WIDTH], kv_all[:, KV_WIDTH:],
                     dist, mask, sinks_ref, o_ref)


def _attn_prompt(h3, kv, mix, sinks, n_prompt_blk):
    t = h3.shape[0]
    halo_per_blk = ROW_BLK // WINDOW
    return pl.pallas_call(
        _attn_prompt_kernel,
        grid=(n_prompt_blk,),
        in_specs=[pl.BlockSpec(memory_space=pltpu.SMEM),
                  pl.BlockSpec((ROW_BLK, Q_WIDTH), lambda i: (i, 2)),
                  pl.BlockSpec((ROW_BLK, 2 * KV_WIDTH), lambda i: (i, 0)),
                  pl.BlockSpec((WINDOW, 2 * KV_WIDTH), lambda i: (jnp.maximum(i * halo_per_blk - 1, 0), 0)),
                  pl.BlockSpec(memory_space=pl.ANY)],
        out_specs=pl.BlockSpec((ROW_BLK, Q_WIDTH), lambda i: (i, 1)),
        out_shape=jax.ShapeDtypeStruct((t, 2 * Q_WIDTH), BF16),
        input_output_aliases={4: 0},
        compiler_params=_params(1),
        name="attn_prompt",
    )(sinks, h3, kv, kv, mix)


def _attn_sample_kernel(n_streams, seq_s, sinks_ref, q_ref, kv_ref, ck_ref, cv_ref, mix_ref, o_ref):
    del mix_ref
    n_cache = ck_ref.shape[0]
    w_c = n_cache // n_streams
    k_all = jnp.concatenate([ck_ref[...], kv_ref[:, :KV_WIDTH]], axis=0).astype(BF16)
    v_all = jnp.concatenate([cv_ref[...], kv_ref[:, KV_WIDTH:]], axis=0).astype(BF16)
    n_keys = n_cache + ROW_BLK
    r = lax.broadcasted_iota(I32, (ROW_BLK, n_keys), 0)
    c = lax.broadcasted_iota(I32, (ROW_BLK, n_keys), 1)
    is_new = c >= n_cache
    k_stream = jnp.where(is_new, (c - n_cache) // seq_s, c // w_c)
    k_pos = jnp.where(is_new, w_c + (c - n_cache) % seq_s, c % w_c)
    dist = jnp.abs(w_c + r % seq_s - k_pos).astype(F32)
    mask = k_stream == r // seq_s
    _attention_heads(q_ref[...].astype(BF16), k_all, v_all, dist, mask, sinks_ref, o_ref)


def _attn_sample(h3, kv, mix, cache_k2d, cache_v2d, sinks, n_prompt_blk, n_streams, seq_s):
    t = h3.shape[0]
    full = lambda i: (0, 0)
    return pl.pallas_call(
        functools.partial(_attn_sample_kernel, n_streams, seq_s),
        grid=(1,),
        in_specs=[pl.BlockSpec(memory_space=pltpu.SMEM),
                  pl.BlockSpec((ROW_BLK, Q_WIDTH), lambda i: (n_prompt_blk, 2)),
                  pl.BlockSpec((ROW_BLK, 2 * KV_WIDTH), lambda i: (n_prompt_blk, 0)),
                  pl.BlockSpec(cache_k2d.shape, full),
                  pl.BlockSpec(cache_v2d.shape, full),
                  pl.BlockSpec(memory_space=pl.ANY)],
        out_specs=pl.BlockSpec((ROW_BLK, Q_WIDTH), lambda i: (n_prompt_blk, 1)),
        out_shape=jax.ShapeDtypeStruct((t, 2 * Q_WIDTH), BF16),
        input_output_aliases={5: 0},
        compiler_params=_params(1),
        name="attn_sample",
    )(sinks, h3, kv, cache_k2d, cache_v2d, mix)


def _conv_kernel(n_prompt_blk, n_streams, seq_s, z_ref, halo_ref, go_ref, st_ref, cw_ref, o_ref):
    i = pl.program_id(0)
    z = z_ref[...]
    rows = lax.broadcasted_iota(I32, (ROW_BLK, 1), 0)
    zm1 = pltpu.roll(z, 1, 0)
    zm2 = pltpu.roll(z, 2, 0)

    def finish(zm1, zm2):
        conv = cw_ref[0:1, :] * zm2 + cw_ref[1:2, :] * zm1 + cw_ref[2:3, :] * z
        o_ref[...] = (go_ref[...] * conv).astype(BF16)

    @pl.when(i < n_prompt_blk)
    def _():
        live = jnp.where(i > 0, 1.0, 0.0)
        p1 = halo_ref[7:8, :] * live
        p2 = halo_ref[6:7, :] * live
        finish(jnp.where(rows == 0, p1, zm1),
               jnp.where(rows == 0, p2, jnp.where(rows == 1, p1, zm2)))

    @pl.when(i == n_prompt_blk)
    def _():
        a, b = zm1, zm2
        for s in range(n_streams):
            s0 = st_ref[2 * s:2 * s + 1, :]
            s1 = st_ref[2 * s + 1:2 * s + 2, :]
            a = jnp.where(rows == s * seq_s, s1, a)
            b = jnp.where(rows == s * seq_s, s0, jnp.where(rows == s * seq_s + 1, s1, b))
        finish(a, b)


def _gated_conv(z, gate_out, state2d, conv_w, n_prompt_blk, n_streams, seq_s):
    t = z.shape[0]
    halo_per_blk = ROW_BLK // 8
    return pl.pallas_call(
        functools.partial(_conv_kernel, n_prompt_blk, n_streams, seq_s),
        grid=(t // ROW_BLK,),
        in_specs=[pl.BlockSpec((ROW_BLK, D_MODEL), lambda i: (i, 0)),
                  pl.BlockSpec((8, D_MODEL), lambda i: (jnp.maximum(i * halo_per_blk - 1, 0), 0)),
                  pl.BlockSpec((ROW_BLK, D_MODEL), lambda i: (i, 0)),
                  pl.BlockSpec(state2d.shape, lambda i: (0, 0)),
                  pl.BlockSpec(conv_w.shape, lambda i: (0, 0))],
        out_specs=pl.BlockSpec((ROW_BLK, D_MODEL), lambda i: (i, 0)),
        out_shape=jax.ShapeDtypeStruct((t, D_MODEL), BF16),
        compiler_params=_params(1),
        name="gated_conv",
    )(z, z, gate_out, state2d, conv_w)


def _split_bf16(x):
    hi = x.astype(BF16)
    lo = (x - hi.astype(F32)).astype(BF16)
    return hi, lo


def _router_kernel(n_blk, x_ref, w_ref, br_ref, ri_ref, gate_ref, cnt_ref, carry_ref):
    i = pl.program_id(0)

    @pl.when(i == 0)
    def _():
        carry_ref[...] = jnp.zeros_like(carry_ref)

    xh, xl = _split_bf16(x_ref[...])
    wh, wl = _split_bf16(w_ref[...])
    logits = (jnp.dot(xh, wh, preferred_element_type=F32) + jnp.dot(xh, wl, preferred_element_type=F32)
              + jnp.dot(xl, wh, preferred_element_type=F32))
    lane = lax.broadcasted_iota(I32, (1, LANES), 1)
    logits = jnp.where(lane < N_EXPERTS, logits, NEG_INF)
    ex = jnp.exp(logits - jnp.max(logits, axis=-1, keepdims=True))
    scores = ex / jnp.sum(ex, axis=-1, keepdims=True)
    biased = scores + br_ref[...]
    sc = [scores[:, e:e + 1] for e in range(N_EXPERTS)]
    bs = [biased[:, e:e + 1] for e in range(N_EXPERTS)]

    gscore = []
    for g in range(N_GROUPS):
        a, b, c, d = bs[4 * g:4 * g + 4]
        hi1, lo1 = jnp.maximum(a, b), jnp.minimum(a, b)
        hi2, lo2 = jnp.maximum(c, d), jnp.minimum(c, d)
        gscore.append(jnp.maximum(hi1, hi2) + jnp.maximum(jnp.minimum(hi1, hi2), jnp.maximum(lo1, lo2)))
    best, grp = gscore[0], jnp.zeros((ROW_BLK, 1), I32)
    for g in range(1, N_GROUPS):
        better = gscore[g] > best
        grp = jnp.where(better, g, grp)
        best = jnp.where(better, gscore[g], best)

    def pick(cols, idx, n):
        out = cols[0]
        for j in range(1, n):
            out = jnp.where(idx == j, cols[j], out)
        return out

    in_b = [pick([bs[4 * g + j] for g in range(N_GROUPS)], grp, N_GROUPS) for j in range(EXPERTS_PER_GROUP)]
    in_s = [pick([sc[4 * g + j] for g in range(N_GROUPS)], grp, N_GROUPS) for j in range(EXPERTS_PER_GROUP)]
    v1, l1 = in_b[0], jnp.zeros((ROW_BLK, 1), I32)
    for j in range(1, EXPERTS_PER_GROUP):
        better = in_b[j] > v1
        l1 = jnp.where(better, j, l1)
        v1 = jnp.where(better, in_b[j], v1)
    v2, l2 = jnp.full((ROW_BLK, 1), -jnp.inf, F32), jnp.zeros((ROW_BLK, 1), I32)
    for j in range(EXPERTS_PER_GROUP):
        better = (l1 != j) & (in_b[j] > v2)
        l2 = jnp.where(better, j, l2)
        v2 = jnp.where(better, in_b[j], v2)
    gate1 = pick(in_s, l1, EXPERTS_PER_GROUP)
    gate2 = pick(in_s, l2, EXPERTS_PER_GROUP)
    gsum = gate1 + gate2
    gate_ref[:, 0:1] = gate1 / gsum
    gate_ref[:, 1:2] = gate2 / gsum
    e1 = grp * EXPERTS_PER_GROUP + l1
    e2 = grp * EXPERTS_PER_GROUP + l2
    ri_ref[:, 0:1] = e1
    ri_ref[:, 1:2] = e2

    oh1 = (lane == e1).astype(F32)
    oh2 = (lane == e2).astype(F32)
    r = lax.broadcasted_iota(I32, (ROW_BLK, ROW_BLK), 0)
    c = lax.broadcasted_iota(I32, (ROW_BLK, ROW_BLK), 1)
    before = (c < r).astype(BF16)
    cs = jnp.dot(before, jnp.concatenate([oh1, oh2], axis=1).astype(BF16), preferred_element_type=F32)
    tot1 = jnp.sum(oh1, axis=0, keepdims=True)
    tot2 = jnp.sum(oh2, axis=0, keepdims=True)
    carry = carry_ref[...]
    ri_ref[:, 2:3] = jnp.sum(oh1 * (carry + cs[:, :LANES]), axis=-1, keepdims=True).astype(I32)
    ri_ref[:, 3:4] = jnp.sum(oh2 * (carry + tot1 + cs[:, LANES:]), axis=-1, keepdims=True).astype(I32)
    carry_ref[...] = carry + tot1 + tot2

    @pl.when(i == n_blk - 1)
    def _():
        cnt_ref[...] = jnp.broadcast_to(carry_ref[...], cnt_ref.shape)


def _router(x, w_pad, br_pad):
    t = x.shape[0]
    n_blk = t // ROW_BLK
    return pl.pallas_call(
        functools.partial(_router_kernel, n_blk),
        grid=(n_blk,),
        in_specs=[pl.BlockSpec((ROW_BLK, D_MODEL), lambda i: (i, 0)),
                  pl.BlockSpec((D_MODEL, LANES), lambda i: (0, 0)),
                  pl.BlockSpec((1, LANES), lambda i: (0, 0))],
        out_specs=[pl.BlockSpec((ROW_BLK, 4), lambda i: (i, 0)),
                   pl.BlockSpec((ROW_BLK, 2), lambda i: (i, 0)),
                   pl.BlockSpec((8, LANES), lambda i: (0, 0))],
        out_shape=[jax.ShapeDtypeStruct((t, 4), I32),
                   jax.ShapeDtypeStruct((t, 2), F32),
                   jax.ShapeDtypeStruct((8, LANES), F32)],
        scratch_shapes=[pltpu.VMEM((1, LANES), F32)],
        compiler_params=_params(1),
        name="router",
    )(x, w_pad, br_pad)


DISPATCH_CHUNK = 256


def _dispatch_kernel(n_tok, pos_ref, meta_ref, x_ref, xs_ref, sem):
    def row_copy(src_row, dst_row):
        return pltpu.make_async_copy(x_ref.at[pl.ds(src_row, 1)], xs_ref.at[pl.ds(dst_row, 1)], sem)

    def wait_chunk():
        rows = pl.ds(0, 2 * DISPATCH_CHUNK)
        pltpu.make_async_copy(xs_ref.at[rows], xs_ref.at[rows], sem).wait()

    def chunk(ci, carry):
        def tok(ti, carry):
            t = ci * DISPATCH_CHUNK + ti
            row_copy(t, pos_ref[2 * t]).start()
            row_copy(t, pos_ref[2 * t + 1]).start()
            return carry

        lax.fori_loop(0, DISPATCH_CHUNK, tok, 0, unroll=8)

        @pl.when(ci > 0)
        def _():
            wait_chunk()

        return carry

    lax.fori_loop(0, n_tok // DISPATCH_CHUNK, chunk, 0)
    wait_chunk()

    for e in range(N_EXPERTS):
        cnt, start, padded = meta_ref[e], meta_ref[N_EXPERTS + e], meta_ref[2 * N_EXPERTS + e]

        def fill(ri, carry):
            row_copy(0, start + ri).start()
            return carry

        def drain(ri, carry):
            row_copy(0, 0).wait()
            return carry

        lax.fori_loop(cnt, padded, fill, 0)
        lax.fori_loop(cnt, padded, drain, 0)


def _dispatch(pos_flat, meta, x, n_rows):
    t = x.shape[0]
    return pl.pallas_call(
        functools.partial(_dispatch_kernel, t),
        in_specs=[pl.BlockSpec(memory_space=pltpu.SMEM),
                  pl.BlockSpec(memory_space=pltpu.SMEM),
                  pl.BlockSpec(memory_space=pl.ANY)],
        out_specs=pl.BlockSpec(memory_space=pl.ANY),
        out_shape=jax.ShapeDtypeStruct((n_rows, D_MODEL), x.dtype),
        scratch_shapes=[pltpu.SemaphoreType.DMA(())],
        name="dispatch",
    )(pos_flat, meta, x)


def _expert_mlp_kernel(te_ref, nu_ref, x_ref, wg_ref, wu_ref, wd_ref, o_ref):
    del te_ref
    i = pl.program_id(0)
    j = pl.program_id(1)

    @pl.when(i < nu_ref[0])
    def _():
        xb = x_ref[...].astype(BF16)
        g = jnp.dot(xb, wg_ref[0, 0].astype(BF16), preferred_element_type=F32)
        u = jnp.dot(xb, wu_ref[0, 0].astype(BF16), preferred_element_type=F32)
        h = (g * jax.nn.sigmoid(g) * u).astype(BF16)
        part = jnp.dot(h, wd_ref[0, 0].astype(BF16), preferred_element_type=F32)

        @pl.when(j == 0)
        def _():
            o_ref[...] = part

        @pl.when(j > 0)
        def _():
            o_ref[...] += part


def _expert_mlp(tile_expert, n_used, xs, w_gate, w_up, w_down, li):
    n_rows = xs.shape[0]
    n_tiles = n_rows // MOE_ROWS
    nj = D_EXPERT // MOE_FCHUNK

    def row_map(i, j, te, nu):
        return (jnp.minimum(i, nu[0] - 1), 0)

    def col_chunk(i, j, nu):
        return jnp.where(i < nu[0], j, nj - 1)

    grid_spec = pltpu.PrefetchScalarGridSpec(
        num_scalar_prefetch=2,
        grid=(n_tiles, nj),
        in_specs=[pl.BlockSpec((MOE_ROWS, D_MODEL), row_map),
                  pl.BlockSpec((1, 1, D_MODEL, MOE_FCHUNK),
                               lambda i, j, te, nu: (li, te[i], 0, col_chunk(i, j, nu))),
                  pl.BlockSpec((1, 1, D_MODEL, MOE_FCHUNK),
                               lambda i, j, te, nu: (li, te[i], 0, col_chunk(i, j, nu))),
                  pl.BlockSpec((1, 1, MOE_FCHUNK, D_MODEL),
                               lambda i, j, te, nu: (li, te[i], col_chunk(i, j, nu), 0))],
        out_specs=pl.BlockSpec((MOE_ROWS, D_MODEL), row_map),
    )
    return pl.pallas_call(
        _expert_mlp_kernel,
        grid_spec=grid_spec,
        out_shape=jax.ShapeDtypeStruct((n_rows, D_MODEL), F32),
        compiler_params=_params(2),
        name="expert_mlp",
    )(tile_expert, n_used, xs, w_gate, w_up, w_down)


def _combine_kernel(pos_ref, ys_ref, x_ref, gate_ref, g_ref, b_ref, of_ref, ob_ref, buf_ref, sem):
    base = pl.program_id(0) * ROW_BLK

    def tok(ti, carry):
        t = base + ti
        for k in range(2):
            pltpu.make_async_copy(ys_ref.at[pl.ds(pos_ref[2 * t + k], 1)],
                                  buf_ref.at[k, pl.ds(ti, 1)], sem).start()
        return carry

    lax.fori_loop(0, ROW_BLK, tok, 0, unroll=8)
    for k in range(2):
        pltpu.make_async_copy(ys_ref.at[pl.ds(0, ROW_BLK)], buf_ref.at[k], sem).wait()
    gates = gate_ref[...]
    y = ALPHA * x_ref[...] + gates[:, 0:1] * buf_ref[0] + gates[:, 1:2] * buf_ref[1]
    y = _layer_norm(y, g_ref[...], b_ref[...])
    of_ref[...] = y
    ob_ref[...] = y.astype(BF16)


def _combine_ln(pos_flat, ys, x, gates, g, b):
    t = x.shape[0]
    row = lambda i, pos: (i, 0)
    vec = lambda i, pos: (0, 0)
    grid_spec = pltpu.PrefetchScalarGridSpec(
        num_scalar_prefetch=1,
        grid=(t // ROW_BLK,),
        in_specs=[pl.BlockSpec(memory_space=pl.ANY),
                  pl.BlockSpec((ROW_BLK, D_MODEL), row),
                  pl.BlockSpec((ROW_BLK, 2), row),
                  pl.BlockSpec((1, D_MODEL), vec),
                  pl.BlockSpec((1, D_MODEL), vec)],
        out_specs=[pl.BlockSpec((ROW_BLK, D_MODEL), row),
                   pl.BlockSpec((ROW_BLK, D_MODEL), row)],
        scratch_shapes=[pltpu.VMEM((2, ROW_BLK, D_MODEL), F32), pltpu.SemaphoreType.DMA(())],
    )
    return pl.pallas_call(
        _combine_kernel,
        grid_spec=grid_spec,
        out_shape=[jax.ShapeDtypeStruct((t, D_MODEL), F32),
                   jax.ShapeDtypeStruct((t, D_MODEL), BF16)],
        compiler_params=_params(1),
        name="combine_ln",
    )(pos_flat, ys, x, gates, g.reshape(1, -1), b.reshape(1, -1))


def _moe_layer(x, w_pad, br_pad, w_gate, w_up, w_down, li, ln_g, ln_b):
    t = x.shape[0]
    n_tiles = -(-(2 * t + N_EXPERTS * (MOE_ROWS - 1)) // MOE_ROWS)
    ri, gates, cnt = _router(x, w_pad, br_pad)
    counts = cnt[0, :N_EXPERTS].astype(I32)
    padded = (counts + MOE_ROWS - 1) // MOE_ROWS * MOE_ROWS
    ends = jnp.cumsum(padded)
    starts = ends - padded
    onehot = ri[:, 0:2, None] == jnp.arange(N_EXPERTS, dtype=I32)
    pos = jnp.sum(jnp.where(onehot, starts, 0), axis=-1) + ri[:, 2:4]
    pos_flat = pos.reshape(-1)
    n_used = ends[-1] // MOE_ROWS
    tile_row = jnp.minimum(jnp.arange(n_tiles, dtype=I32), n_used - 1) * MOE_ROWS
    tile_expert = jnp.minimum(jnp.sum(tile_row[:, None] >= ends[None, :], axis=1), N_EXPERTS - 1).astype(I32)
    meta = jnp.concatenate([counts, starts, padded]).astype(I32)

    xs = _dispatch(pos_flat, meta, x, n_tiles * MOE_ROWS)
    ys = _expert_mlp(tile_expert, n_used.reshape(1).astype(I32), xs, w_gate, w_up, w_down, li)
    return _combine_ln(pos_flat, ys, x, gates, ln_g, ln_b)


def kernel(x_prompt, x_sample, cache_k, cache_v, state_conv, w_in_even, ln_v_g, ln_v_b, w_spatial, b_spatial, sinks, w_out_even, w_in_odd, conv_w, w_out_odd, ln_mix_g, ln_mix_b, ln_ffn_g, ln_ffn_b, w_router, b_router, w_gate, w_up, w_down):
    n_batch, seq_p, _ = x_prompt.shape
    n_streams, seq_s, _ = x_sample.shape
    n_p = n_batch * seq_p
    n_s = n_streams * seq_s
    assert n_batch == 1 and n_s == ROW_BLK and n_p % ROW_BLK == 0 and (n_p + n_s) % MM_ROWS == 0
    assert seq_s & (seq_s - 1) == 0 and A_CHUNK % seq_s == 0 and cache_k.shape[2] == WINDOW
    n_prompt_blk = n_p // ROW_BLK
    win_p = min(WINDOW, seq_p)

    x = jnp.concatenate([x_prompt.reshape(n_p, D_MODEL), x_sample.reshape(n_s, D_MODEL)], axis=0)
    xb = x.astype(BF16)
    w_pad = jnp.pad(w_router, ((0, 0), (0, LANES - N_EXPERTS)))
    br_pad = jnp.pad(b_router, (0, LANES - N_EXPERTS)).reshape(1, LANES)
    reps = A_CHUNK // seq_s

    k_p, v_p, k_s, v_s, vg_s, c_p, c_s = [], [], [], [], [], [], []
    for layer in range(DEPTH):
        i = layer // 2
        if layer % 2 == 0:
            h3 = _matmul_cols(xb, w_in_even, i, 0, 6, 512, F32)
            kv = _matmul_cols(xb, w_in_even, i, 12, 1, 2 * KV_WIDTH, F32)
            w_stack = jnp.stack([w_spatial[i], jnp.tile(w_spatial[i][:, :seq_s, :seq_s], (1, reps, reps))])
            bt_stack = jnp.stack([b_spatial[i].T, jnp.tile(b_spatial[i][:, :seq_s], (1, reps)).T])
            mix, vn_s = _mixer_a(h3, w_stack, bt_stack, ln_v_g[i], ln_v_b[i], n_prompt_blk, seq_s)
            mix = _attn_prompt(h3, kv, mix, sinks[i], n_prompt_blk)
            mix = _attn_sample(h3, kv, mix, cache_k[i].reshape(n_streams * WINDOW, KV_WIDTH),
                               cache_v[i].reshape(n_streams * WINDOW, KV_WIDTH), sinks[i],
                               n_prompt_blk, n_streams, seq_s)
            x, xb = _out_proj_ln(mix, w_out_even, i, x, ln_mix_g[layer], ln_mix_b[layer])
            k_p.append(kv[n_p - win_p:n_p, :KV_WIDTH].reshape(n_batch, win_p, N_KV_HEADS, HEAD_DIM))
            v_p.append(kv[n_p - win_p:n_p, KV_WIDTH:].reshape(n_batch, win_p, N_KV_HEADS, HEAD_DIM))
            k_s.append(kv[n_p:, :KV_WIDTH].reshape(n_streams, seq_s, N_KV_HEADS, HEAD_DIM))
            v_s.append(kv[n_p:, KV_WIDTH:].reshape(n_streams, seq_s, N_KV_HEADS, HEAD_DIM))
            vg_s.append(vn_s.reshape(n_streams, seq_s, A_WIDTH))
        else:
            gate_out, z = _odd_in_proj(xb, w_in_odd, i)
            g = _gated_conv(z, gate_out, state_conv[i].reshape(n_streams * (CONV_WIDTH - 1), D_MODEL),
                            conv_w[i], n_prompt_blk, n_streams, seq_s)
            x, xb = _out_proj_ln(g, w_out_odd, i, x, ln_mix_g[layer], ln_mix_b[layer])
            c_p.append(z[n_p - (CONV_WIDTH - 1):n_p].reshape(n_batch, CONV_WIDTH - 1, D_MODEL))
            c_s.append(z[n_p:].reshape(n_streams, seq_s, D_MODEL)[:, seq_s - (CONV_WIDTH - 1):])
        x, xb = _moe_layer(x, w_pad, br_pad, w_gate, w_up, w_down, layer,
                           ln_ffn_g[layer], ln_ffn_b[layer])
    return (x[:n_p].reshape(n_batch, seq_p, D_MODEL), x[n_p:].reshape(n_streams, seq_s, D_MODEL),
            jnp.stack(k_p), jnp.stack(v_p), jnp.stack(k_s), jnp.stack(v_s),
            jnp.stack(vg_s), jnp.stack(c_p), jnp.stack(c_s))
```

```python
import functools

import jax
import jax.numpy as jnp
from jax import lax
from jax.experimental import pallas as pl
from jax.experimental.pallas import tpu as pltpu

F32 = jnp.float32
BF16 = jnp.bfloat16
I32 = jnp.int32

D_MODEL = 2048
DEPTH = 4
CHUNK = 64
A_HEADS = 8
A_HEAD_DIM = 128
A_WIDTH = A_HEADS * A_HEAD_DIM
A_CHUNK = 128
N_HEADS = 16
N_KV_HEADS = 2
HEAD_DIM = 64
Q_WIDTH = N_HEADS * HEAD_DIM
KV_WIDTH = N_KV_HEADS * HEAD_DIM
WINDOW = 128
CONV_WIDTH = 3
N_EXPERTS = 16
N_GROUPS = 4
EXPERTS_PER_GROUP = N_EXPERTS // N_GROUPS
D_EXPERT = 1024
ALPHA = (2 * DEPTH) ** 0.25
LN_EPS = 1e-5
NEG_INF = -1e30
ALIBI_SLOPES = tuple(2.0 ** (-8.0 * h / N_HEADS) for h in range(1, N_HEADS + 1))

LANES = 128
ROW_BLK = 256
MM_ROWS = 768
OUT_ROWS = 384
MOE_ROWS = 256
W_QUARTERS = 4
VMEM_LIMIT = 56 * 1024 * 1024


def _params(n_axes, **kw):
    return pltpu.CompilerParams(dimension_semantics=("arbitrary",) * n_axes,
                                vmem_limit_bytes=VMEM_LIMIT, **kw)


def _layer_norm(y, g, b):
    mu = jnp.mean(y, axis=-1, keepdims=True)
    yc = y - mu
    var = jnp.mean(yc * yc, axis=-1, keepdims=True)
    return yc * lax.rsqrt(var + LN_EPS) * g + b


def _matmul_cols_kernel(x_ref, w_ref, o_ref, wb_ref):
    @pl.when(pl.program_id(1) == 0)
    def _():
        wb_ref[...] = w_ref[0].astype(BF16)

    o_ref[...] = jnp.dot(x_ref[...], wb_ref[...], preferred_element_type=F32).astype(o_ref.dtype)


def _matmul_cols(xb, w, li, col_blk0, n_tiles, tn, out_dtype):
    t, k = xb.shape
    return pl.pallas_call(
        _matmul_cols_kernel,
        grid=(n_tiles, t // MM_ROWS),
        in_specs=[pl.BlockSpec((MM_ROWS, k), lambda j, i: (i, 0)),
                  pl.BlockSpec((1, k, tn), lambda j, i: (li, 0, col_blk0 + j))],
        out_specs=pl.BlockSpec((MM_ROWS, tn), lambda j, i: (i, j)),
        out_shape=jax.ShapeDtypeStruct((t, n_tiles * tn), out_dtype),
        scratch_shapes=[pltpu.VMEM((k, tn), BF16)],
        compiler_params=_params(2),
        name="matmul_cols",
    )(xb, w)


def _odd_in_kernel(x_ref, wo_ref, wi_ref, wh_ref, go_ref, z_ref, wb_ref):
    @pl.when(pl.program_id(1) == 0)
    def _():
        wb_ref[0] = wo_ref[0].astype(BF16)
        wb_ref[1] = wi_ref[0].astype(BF16)
        wb_ref[2] = wh_ref[0].astype(BF16)

    x = x_ref[...]
    go_ref[...] = jnp.dot(x, wb_ref[0], preferred_element_type=F32)
    gate_in = jnp.dot(x, wb_ref[1], preferred_element_type=F32)
    h = jnp.dot(x, wb_ref[2], preferred_element_type=F32)
    z_ref[...] = gate_in * h


def _odd_in_proj(xb, w_in, li):
    t, k = xb.shape
    tn = 256
    nt = D_MODEL // tn
    out = jax.ShapeDtypeStruct((t, D_MODEL), F32)
    return pl.pallas_call(
        _odd_in_kernel,
        grid=(nt, t // MM_ROWS),
        in_specs=[pl.BlockSpec((MM_ROWS, k), lambda j, i: (i, 0)),
                  pl.BlockSpec((1, k, tn), lambda j, i: (li, 0, j)),
                  pl.BlockSpec((1, k, tn), lambda j, i: (li, 0, nt + j)),
                  pl.BlockSpec((1, k, tn), lambda j, i: (li, 0, 2 * nt + j))],
        out_specs=[pl.BlockSpec((MM_ROWS, tn), lambda j, i: (i, j)),
                   pl.BlockSpec((MM_ROWS, tn), lambda j, i: (i, j))],
        out_shape=[out, out],
        scratch_shapes=[pltpu.VMEM((3, k, tn), BF16)],
        compiler_params=_params(2),
        name="odd_in_proj",
    )(xb, w_in, w_in, w_in)


def _out_proj_kernel(li, l_ref, w_ref, r_ref, g_ref, b_ref, of_ref, ob_ref, wb_ref, stage_ref, sem):
    kdim = wb_ref.shape[0]
    rows = stage_ref.shape[1]
    n_chunks = kdim // rows

    @pl.when(pl.program_id(0) == 0)
    def _():
        def chunk_copy(c):
            return pltpu.make_async_copy(w_ref.at[li, pl.ds(c * rows, rows)], stage_ref.at[c % 2], sem.at[c % 2])

        chunk_copy(0).start()
        for c in range(n_chunks):
            if c + 1 < n_chunks:
                chunk_copy(c + 1).start()
            chunk_copy(c).wait()
            wb_ref[c * rows:(c + 1) * rows, :] = stage_ref[c % 2].astype(BF16)

    y = ALPHA * r_ref[...] + jnp.dot(l_ref[...], wb_ref[...], preferred_element_type=F32)
    y = _layer_norm(y, g_ref[...], b_ref[...])
    of_ref[...] = y
    ob_ref[...] = y.astype(BF16)


def _out_proj_ln(lhs, w, li, resid, g, b):
    t, kdim = lhs.shape
    row = lambda i: (i, 0)
    vec = lambda i: (0, 0)
    return pl.pallas_call(
        functools.partial(_out_proj_kernel, li),
        grid=(t // OUT_ROWS,),
        in_specs=[pl.BlockSpec((OUT_ROWS, kdim), row),
                  pl.BlockSpec(memory_space=pl.ANY),
                  pl.BlockSpec((OUT_ROWS, D_MODEL), row),
                  pl.BlockSpec((1, D_MODEL), vec),
                  pl.BlockSpec((1, D_MODEL), vec)],
        out_specs=[pl.BlockSpec((OUT_ROWS, D_MODEL), row),
                   pl.BlockSpec((OUT_ROWS, D_MODEL), row)],
        out_shape=[jax.ShapeDtypeStruct((t, D_MODEL), F32),
                   jax.ShapeDtypeStruct((t, D_MODEL), BF16)],
        scratch_shapes=[pltpu.VMEM((kdim, D_MODEL), BF16),
                        pltpu.VMEM((2, kdim // 4, D_MODEL), F32),
                        pltpu.SemaphoreType.DMA((2,))],
        compiler_params=_params(1),
        name="out_proj_ln",
    )(lhs, w, resid, g.reshape(1, -1), b.reshape(1, -1))


def _mixer_a_kernel(n_prompt_blk, seq_s, u_ref, v_ref, w_ref, bt_ref, g_ref, b_ref, a_ref, vn_ref):
    i = pl.program_id(0)
    is_sample = i == n_prompt_blk
    vn = _layer_norm(v_ref[...], g_ref[...], b_ref[...])

    @pl.when(is_sample)
    def _():
        vn_ref[...] = vn

    shift = jnp.where(is_sample, seq_s.bit_length() - 1, A_CHUNK.bit_length() - 1)
    low = jnp.where(is_sample, seq_s - 1, A_CHUNK - 1)
    r = lax.broadcasted_iota(I32, (A_CHUNK, A_CHUNK), 0)
    c = lax.broadcasted_iota(I32, (A_CHUNK, A_CHUNK), 1)
    mask = ((r >> shift) == (c >> shift)) & ((c & low) <= (r & low))
    vb = vn.astype(BF16)
    bt = bt_ref[0]
    for h in range(A_HEADS):
        wm = jnp.where(mask, w_ref[0, h], 0.0).astype(BF16)
        bcol = bt[:, h:h + 1]
        cs = slice(h * A_HEAD_DIM, (h + 1) * A_HEAD_DIM)
        for n in range(ROW_BLK // A_CHUNK):
            rs = slice(n * A_CHUNK, (n + 1) * A_CHUNK)
            s = jnp.dot(wm, vb[rs, cs], preferred_element_type=F32) + bcol
            a_ref[rs, cs] = (u_ref[rs, cs] * s).astype(BF16)


def _mixer_a(h3, w_stack, bt_stack, ln_g, ln_b, n_prompt_blk, seq_s):
    t = h3.shape[0]
    return pl.pallas_call(
        functools.partial(_mixer_a_kernel, n_prompt_blk, seq_s),
        grid=(t // ROW_BLK,),
        in_specs=[pl.BlockSpec((ROW_BLK, A_WIDTH), lambda i: (i, 0)),
                  pl.BlockSpec((ROW_BLK, A_WIDTH), lambda i: (i, 1)),
                  pl.BlockSpec((1, A_HEADS, A_CHUNK, A_CHUNK), lambda i: (i // n_prompt_blk, 0, 0, 0)),
                  pl.BlockSpec((1, A_CHUNK, A_HEADS), lambda i: (i // n_prompt_blk, 0, 0)),
                  pl.BlockSpec((1, A_WIDTH), lambda i: (0, 0)),
                  pl.BlockSpec((1, A_WIDTH), lambda i: (0, 0))],
        out_specs=[pl.BlockSpec((ROW_BLK, A_WIDTH), lambda i: (i, 0)),
                   pl.BlockSpec((ROW_BLK, A_WIDTH), lambda i: (0, 0))],
        out_shape=[jax.ShapeDtypeStruct((t, 2 * A_WIDTH), BF16),
                   jax.ShapeDtypeStruct((ROW_BLK, A_WIDTH), F32)],
        compiler_params=_params(1),
        name="mixer_a",
    )(h3, h3, w_stack, bt_stack, ln_g.reshape(1, -1), ln_b.reshape(1, -1))


def _attention_heads(q, k, v, dist, mask, sinks_ref, o_ref):
    grp = N_HEADS // N_KV_HEADS
    for kv in range(N_KV_HEADS):
        kg = k[:, kv * HEAD_DIM:(kv + 1) * HEAD_DIM]
        vg = v[:, kv * HEAD_DIM:(kv + 1) * HEAD_DIM]
        outs = []
        for j in range(grp):
            h = kv * grp + j
            qh = q[:, h * HEAD_DIM:(h + 1) * HEAD_DIM]
            s = lax.dot_general(qh, kg, (((1,), (1,)), ((), ())), preferred_element_type=F32)
            s = jnp.where(mask, s * (HEAD_DIM ** -0.5) - ALIBI_SLOPES[h] * dist, NEG_INF)
            sink = sinks_ref[h]
            m = jnp.maximum(jnp.max(s, axis=-1, keepdims=True), sink)
            p = jnp.exp(s - m)
            den = jnp.sum(p, axis=-1, keepdims=True) + jnp.exp(sink - m)
            outs.append(jnp.dot(p.astype(BF16), vg, preferred_element_type=F32) / den)
        for j in range(0, grp, 2):
            c0 = (kv * grp + j) * HEAD_DIM
            o_ref[:, c0:c0 + 2 * HEAD_DIM] = jnp.concatenate(outs[j:j + 2], axis=1).astype(BF16)


def _attn_prompt_kernel(sinks_ref, q_ref, kv_ref, halo_ref, mix_ref, o_ref):
    del mix_ref
    i = pl.program_id(0)
    kv_all = jnp.concatenate([halo_ref[...], kv_ref[...]], axis=0).astype(BF16)
    n_keys = WINDOW + ROW_BLK
    r = lax.broadcasted_iota(I32, (ROW_BLK, n_keys), 0)
    c = lax.broadcasted_iota(I32, (ROW_BLK, n_keys), 1)
    dist = jnp.abs(r + WINDOW - c).astype(F32)
    qc = r // CHUNK
    kc = c // CHUNK
    mask = (kc >= qc) & (kc <= qc + WINDOW // CHUNK) & (c >= jnp.where(i > 0, 0, WINDOW))
    _attention_heads(q_ref[...].astype(BF16), kv_all[:, :KV_WIDTH], kv_all[:, KV_WIDTH:],
                     dist, mask, sinks_ref, o_ref)


def _attn_prompt(h3, kv, mix, sinks, n_prompt_blk):
    t = h3.shape[0]
    halo_per_blk = ROW_BLK // WINDOW
    return pl.pallas_call(
        _attn_prompt_kernel,
        grid=(n_prompt_blk,),
        in_specs=[pl.BlockSpec(memory_space=pltpu.SMEM),
                  pl.BlockSpec((ROW_BLK, Q_WIDTH), lambda i: (i, 2)),
                  pl.BlockSpec((ROW_BLK, 2 * KV_WIDTH), lambda i: (i, 0)),
                  pl.BlockSpec((WINDOW, 2 * KV_WIDTH), lambda i: (jnp.maximum(i * halo_per_blk - 1, 0), 0)),
                  pl.BlockSpec(memory_space=pl.ANY)],
        out_specs=pl.BlockSpec((ROW_BLK, Q_WIDTH), lambda i: (i, 1)),
        out_shape=jax.ShapeDtypeStruct((t, 2 * Q_WIDTH), BF16),
        input_output_aliases={4: 0},
        compiler_params=_params(1),
        name="attn_prompt",
    )(sinks, h3, kv, kv, mix)


def _attn_sample_kernel(n_streams, seq_s, sinks_ref, q_ref, kv_ref, ck_ref, cv_ref, mix_ref, o_ref):
    del mix_ref
    n_cache = ck_ref.shape[0]
    w_c = n_cache // n_streams
    k_all = jnp.concatenate([ck_ref[...], kv_ref[:, :KV_WIDTH]], axis=0).astype(BF16)
    v_all = jnp.concatenate([cv_ref[...], kv_ref[:, KV_WIDTH:]], axis=0).astype(BF16)
    n_keys = n_cache + ROW_BLK
    r = lax.broadcasted_iota(I32, (ROW_BLK, n_keys), 0)
    c = lax.broadcasted_iota(I32, (ROW_BLK, n_keys), 1)
    is_new = c >= n_cache
    k_stream = jnp.where(is_new, (c - n_cache) // seq_s, c // w_c)
    k_pos = jnp.where(is_new, w_c + (c - n_cache) % seq_s, c % w_c)
    dist = jnp.abs(w_c + r % seq_s - k_pos).astype(F32)
    mask = k_stream == r // seq_s
    _attention_heads(q_ref[...].astype(BF16), k_all, v_all, dist, mask, sinks_ref, o_ref)


def _attn_sample(h3, kv, mix, cache_k2d, cache_v2d, sinks, n_prompt_blk, n_streams, seq_s):
    t = h3.shape[0]
    full = lambda i: (0, 0)
    return pl.pallas_call(
        functools.partial(_attn_sample_kernel, n_streams, seq_s),
        grid=(1,),
        in_specs=[pl.BlockSpec(memory_space=pltpu.SMEM),
                  pl.BlockSpec((ROW_BLK, Q_WIDTH), lambda i: (n_prompt_blk, 2)),
                  pl.BlockSpec((ROW_BLK, 2 * KV_WIDTH), lambda i: (n_prompt_blk, 0)),
                  pl.BlockSpec(cache_k2d.shape, full),
                  pl.BlockSpec(cache_v2d.shape, full),
                  pl.BlockSpec(memory_space=pl.ANY)],
        out_specs=pl.BlockSpec((ROW_BLK, Q_WIDTH), lambda i: (n_prompt_blk, 1)),
        out_shape=jax.ShapeDtypeStruct((t, 2 * Q_WIDTH), BF16),
        input_output_aliases={5: 0},
        compiler_params=_params(1),
        name="attn_sample",
    )(sinks, h3, kv, cache_k2d, cache_v2d, mix)


def _conv_kernel(n_prompt_blk, n_streams, seq_s, z_ref, halo_ref, go_ref, st_ref, cw_ref, o_ref):
    i = pl.program_id(0)
    z = z_ref[...]
    rows = lax.broadcasted_iota(I32, (ROW_BLK, 1), 0)
    zm1 = pltpu.roll(z, 1, 0)
    zm2 = pltpu.roll(z, 2, 0)

    def finish(zm1, zm2):
        conv = cw_ref[0:1, :] * zm2 + cw_ref[1:2, :] * zm1 + cw_ref[2:3, :] * z
        o_ref[...] = (go_ref[...] * conv).astype(BF16)

    @pl.when(i < n_prompt_blk)
    def _():
        live = jnp.where(i > 0, 1.0, 0.0)
        p1 = halo_ref[7:8, :] * live
        p2 = halo_ref[6:7, :] * live
        finish(jnp.where(rows == 0, p1, zm1),
               jnp.where(rows == 0, p2, jnp.where(rows == 1, p1, zm2)))

    @pl.when(i == n_prompt_blk)
    def _():
        a, b = zm1, zm2
        for s in range(n_streams):
            s0 = st_ref[2 * s:2 * s + 1, :]
            s1 = st_ref[2 * s + 1:2 * s + 2, :]
            a = jnp.where(rows == s * seq_s, s1, a)
            b = jnp.where(rows == s * seq_s, s0, jnp.where(rows == s * seq_s + 1, s1, b))
        finish(a, b)


def _gated_conv(z, gate_out, state2d, conv_w, n_prompt_blk, n_streams, seq_s):
    t = z.shape[0]
    halo_per_blk = ROW_BLK // 8
    return pl.pallas_call(
        functools.partial(_conv_kernel, n_prompt_blk, n_streams, seq_s),
        grid=(t // ROW_BLK,),
        in_specs=[pl.BlockSpec((ROW_BLK, D_MODEL), lambda i: (i, 0)),
                  pl.BlockSpec((8, D_MODEL), lambda i: (jnp.maximum(i * halo_per_blk - 1, 0), 0)),
                  pl.BlockSpec((ROW_BLK, D_MODEL), lambda i: (i, 0)),
                  pl.BlockSpec(state2d.shape, lambda i: (0, 0)),
                  pl.BlockSpec(conv_w.shape, lambda i: (0, 0))],
        out_specs=pl.BlockSpec((ROW_BLK, D_MODEL), lambda i: (i, 0)),
        out_shape=jax.ShapeDtypeStruct((t, D_MODEL), BF16),
        compiler_params=_params(1),
        name="gated_conv",
    )(z, z, gate_out, state2d, conv_w)


def _split_bf16(x):
    hi = x.astype(BF16)
    lo = (x - hi.astype(F32)).astype(BF16)
    return hi, lo


def _router_kernel(n_blk, x_ref, w_ref, br_ref, ri_ref, gate_ref, cnt_ref, carry_ref):
    i = pl.program_id(0)

    @pl.when(i == 0)
    def _():
        carry_ref[...] = jnp.zeros_like(carry_ref)

    xh, xl = _split_bf16(x_ref[...])
    wh, wl = _split_bf16(w_ref[...])
    logits = (jnp.dot(xh, wh, preferred_element_type=F32) + jnp.dot(xh, wl, preferred_element_type=F32)
              + jnp.dot(xl, wh, preferred_element_type=F32))
    lane = lax.broadcasted_iota(I32, (1, LANES), 1)
    logits = jnp.where(lane < N_EXPERTS, logits, NEG_INF)
    ex = jnp.exp(logits - jnp.max(logits, axis=-1, keepdims=True))
    scores = ex / jnp.sum(ex, axis=-1, keepdims=True)
    biased = scores + br_ref[...]
    sc = [scores[:, e:e + 1] for e in range(N_EXPERTS)]
    bs = [biased[:, e:e + 1] for e in range(N_EXPERTS)]

    gscore = []
    for g in range(N_GROUPS):
        a, b, c, d = bs[4 * g:4 * g + 4]
        hi1, lo1 = jnp.maximum(a, b), jnp.minimum(a, b)
        hi2, lo2 = jnp.maximum(c, d), jnp.minimum(c, d)
        gscore.append(jnp.maximum(hi1, hi2) + jnp.maximum(jnp.minimum(hi1, hi2), jnp.maximum(lo1, lo2)))
    best, grp = gscore[0], jnp.zeros((ROW_BLK, 1), I32)
    for g in range(1, N_GROUPS):
        better = gscore[g] > best
        grp = jnp.where(better, g, grp)
        best = jnp.where(better, gscore[g], best)

    def pick(cols, idx, n):
        out = cols[0]
        for j in range(1, n):
            out = jnp.where(idx == j, cols[j], out)
        return out

    in_b = [pick([bs[4 * g + j] for g in range(N_GROUPS)], grp, N_GROUPS) for j in range(EXPERTS_PER_GROUP)]
    in_s = [pick([sc[4 * g + j] for g in range(N_GROUPS)], grp, N_GROUPS) for j in range(EXPERTS_PER_GROUP)]
    v1, l1 = in_b[0], jnp.zeros((ROW_BLK, 1), I32)
    for j in range(1, EXPERTS_PER_GROUP):
        better = in_b[j] > v1
        l1 = jnp.where(better, j, l1)
        v1 = jnp.where(better, in_b[j], v1)
    v2, l2 = jnp.full((ROW_BLK, 1), -jnp.inf, F32), jnp.zeros((ROW_BLK, 1), I32)
    for j in range(EXPERTS_PER_GROUP):
        better = (l1 != j) & (in_b[j] > v2)
        l2 = jnp.where(better, j, l2)
        v2 = jnp.where(better, in_b[j], v2)
    gate1 = pick(in_s, l1, EXPERTS_PER_GROUP)
    gate2 = pick(in_s, l2, EXPERTS_PER_GROUP)
    gsum = gate1 + gate2
    gate_ref[:, 0:1] = gate1 / gsum
    gate_ref[:, 1:2] = gate2 / gsum
    e1 = grp * EXPERTS_PER_GROUP + l1
    e2 = grp * EXPERTS_PER_GROUP + l2
    ri_ref[:, 0:1] = e1
    ri_ref[:, 1:2] = e2

    oh1 = (lane == e1).astype(F32)
    oh2 = (lane == e2).astype(F32)
    r = lax.broadcasted_iota(I32, (ROW_BLK, ROW_BLK), 0)
    c = lax.broadcasted_iota(I32, (ROW_BLK, ROW_BLK), 1)
    before = (c < r).astype(BF16)
    cs = jnp.dot(before, jnp.concatenate([oh1, oh2], axis=1).astype(BF16), preferred_element_type=F32)
    tot1 = jnp.sum(oh1, axis=0, keepdims=True)
    tot2 = jnp.sum(oh2, axis=0, keepdims=True)
    carry = carry_ref[...]
    ri_ref[:, 2:3] = jnp.sum(oh1 * (carry + cs[:, :LANES]), axis=-1, keepdims=True).astype(I32)
    ri_ref[:, 3:4] = jnp.sum(oh2 * (carry + tot1 + cs[:, LANES:]), axis=-1, keepdims=True).astype(I32)
    carry_ref[...] = carry + tot1 + tot2

    @pl.when(i == n_blk - 1)
    def _():
        cnt_ref[...] = jnp.broadcast_to(carry_ref[...], cnt_ref.shape)


def _router(x, w_pad, br_pad):
    t = x.shape[0]
    n_blk = t // ROW_BLK
    return pl.pallas_call(
        functools.partial(_router_kernel, n_blk),
        grid=(n_blk,),
        in_specs=[pl.BlockSpec((ROW_BLK, D_MODEL), lambda i: (i, 0)),
                  pl.BlockSpec((D_MODEL, LANES), lambda i: (0, 0)),
                  pl.BlockSpec((1, LANES), lambda i: (0, 0))],
        out_specs=[pl.BlockSpec((ROW_BLK, 4), lambda i: (i, 0)),
                   pl.BlockSpec((ROW_BLK, 2), lambda i: (i, 0)),
                   pl.BlockSpec((8, LANES), lambda i: (0, 0))],
        out_shape=[jax.ShapeDtypeStruct((t, 4), I32),
                   jax.ShapeDtypeStruct((t, 2), F32),
                   jax.ShapeDtypeStruct((8, LANES), F32)],
        scratch_shapes=[pltpu.VMEM((1, LANES), F32)],
        compiler_params=_params(1),
        name="router",
    )(x, w_pad, br_pad)


def _dispatch_kernel(pos_ref, meta_ref, x_ref, xs_ref, sem):
    i = pl.program_id(0)
    base = i * ROW_BLK

    def row_copy(src_row, dst_row):
        return pltpu.make_async_copy(x_ref.at[pl.ds(src_row, 1)], xs_ref.at[pl.ds(dst_row, 1)], sem)

    def tok(ti, carry):
        row_copy(ti, pos_ref[2 * (base + ti)]).start()
        row_copy(ti, pos_ref[2 * (base + ti) + 1]).start()
        return carry

    lax.fori_loop(0, ROW_BLK, tok, 0, unroll=8)

    @pl.when(i == 0)
    def _():
        for e in range(N_EXPERTS):
            cnt, start, padded = meta_ref[e], meta_ref[N_EXPERTS + e], meta_ref[2 * N_EXPERTS + e]

            def fill(ri, carry):
                row_copy(0, start + ri).start()
                return carry

            def drain(ri, carry):
                row_copy(0, 0).wait()
                return carry

            lax.fori_loop(cnt, padded, fill, 0)
            lax.fori_loop(cnt, padded, drain, 0)

    for _ in range(2):
        pltpu.make_async_copy(x_ref, xs_ref.at[pl.ds(0, ROW_BLK)], sem).wait()


def _dispatch(pos_flat, meta, x, n_rows):
    t = x.shape[0]
    grid_spec = pltpu.PrefetchScalarGridSpec(
        num_scalar_prefetch=2,
        grid=(t // ROW_BLK,),
        in_specs=[pl.BlockSpec((ROW_BLK, D_MODEL), lambda i, pos, meta: (i, 0))],
        out_specs=pl.BlockSpec(memory_space=pl.ANY),
        scratch_shapes=[pltpu.SemaphoreType.DMA(())],
    )
    return pl.pallas_call(
        _dispatch_kernel,
        grid_spec=grid_spec,
        out_shape=jax.ShapeDtypeStruct((n_rows, D_MODEL), x.dtype),
        compiler_params=_params(1),
        name="dispatch",
    )(pos_flat, meta, x)


def _expert_mlp_kernel(li, first_ref, nt_ref, row0_ref, slot_ref, next_ref, xs_ref, wg_ref, wu_ref, wd_ref,
                       ys_ref, wgb, wub, wdb, sg, su, sd, xstage, ostage, w_sem, x_sem, y_sem):
    e = pl.program_id(0)
    nt = nt_ref[e]
    row0 = row0_ref[e]
    slot = slot_ref[e]
    e_next = next_ref[e]
    prefetch = e_next >= 0
    w_parts = ((wg_ref, sg, wgb, D_MODEL // W_QUARTERS),
               (wu_ref, su, wub, D_MODEL // W_QUARTERS),
               (wd_ref, sd, wdb, D_EXPERT // W_QUARTERS))

    def w_copy(k, expert, q, s):
        src, stage, _, rows = w_parts[k]
        return pltpu.make_async_copy(src.at[li, expert, pl.ds(pl.multiple_of(q * rows, rows), rows)],
                                     stage.at[s], w_sem.at[k, s])

    def w_quarter(expert, q, dst_slot):
        s = q % 2
        for k in range(3):
            w_copy(k, expert, q, s).wait()

        @pl.when(q + 1 < W_QUARTERS)
        def _():
            for k in range(3):
                w_copy(k, expert, q + 1, 1 - s).start()

        for _, stage, dst, rows in w_parts:
            dst[dst_slot, pl.ds(pl.multiple_of(q * rows, rows), rows), :] = stage[s].astype(BF16)

    def hbm_rows(r):
        return pl.ds(pl.multiple_of(row0 + r * MOE_ROWS, MOE_ROWS), MOE_ROWS)

    def x_load(r, s):
        return pltpu.make_async_copy(xs_ref.at[hbm_rows(r)], xstage.at[s], x_sem.at[s])

    def y_store(r, s):
        return pltpu.make_async_copy(ostage.at[s], ys_ref.at[hbm_rows(r)], y_sem.at[s])

    def tile(r):
        s = r % 2
        x_load(r, s).wait()

        @pl.when(r + 1 < nt)
        def _():
            x_load(r + 1, 1 - s).start()

        xb = xstage[s].astype(BF16)
        g = jnp.dot(xb, wgb[slot], preferred_element_type=F32)
        u = jnp.dot(xb, wub[slot], preferred_element_type=F32)
        h = (g * jax.nn.sigmoid(g) * u).astype(BF16)
        out = jnp.dot(h, wdb[slot], preferred_element_type=F32)

        @pl.when(r >= 2)
        def _():
            y_store(r - 2, s).wait()

        ostage[s] = out
        y_store(r, s).start()

    @pl.when(e == first_ref[0])
    def _():
        for k in range(3):
            w_copy(k, e, 0, 0).start()

        def own(q, carry):
            w_quarter(e, q, slot)
            return carry

        lax.fori_loop(0, W_QUARTERS, own, 0)

    @pl.when(nt > 0)
    def _():
        @pl.when(prefetch)
        def _():
            for k in range(3):
                w_copy(k, e_next, 0, 0).start()

        x_load(0, 0).start()

        def step(r, carry):
            @pl.when(r < nt)
            def _():
                tile(r)

            @pl.when(prefetch & (r < W_QUARTERS))
            def _():
                w_quarter(e_next, r, 1 - slot)

            return carry

        lax.fori_loop(0, jnp.maximum(nt, jnp.where(prefetch, W_QUARTERS, 0)), step, 0)
        y_store(0, (nt - 1) % 2).wait()

        @pl.when(nt >= 2)
        def _():
            y_store(0, nt % 2).wait()


def _expert_mlp(first, tiles_e, starts, slot_e, next_e, xs, w_gate, w_up, w_down, li):
    n_rows = xs.shape[0]
    qg, qd = D_MODEL // W_QUARTERS, D_EXPERT // W_QUARTERS
    grid_spec = pltpu.PrefetchScalarGridSpec(
        num_scalar_prefetch=5,
        grid=(N_EXPERTS,),
        in_specs=[pl.BlockSpec(memory_space=pl.ANY)] * 4,
        out_specs=pl.BlockSpec(memory_space=pl.ANY),
        scratch_shapes=[pltpu.VMEM((2, D_MODEL, D_EXPERT), BF16),
                        pltpu.VMEM((2, D_MODEL, D_EXPERT), BF16),
                        pltpu.VMEM((2, D_EXPERT, D_MODEL), BF16),
                        pltpu.VMEM((2, qg, D_EXPERT), F32),
                        pltpu.VMEM((2, qg, D_EXPERT), F32),
                        pltpu.VMEM((2, qd, D_MODEL), F32),
                        pltpu.VMEM((2, MOE_ROWS, D_MODEL), xs.dtype),
                        pltpu.VMEM((2, MOE_ROWS, D_MODEL), F32),
                        pltpu.SemaphoreType.DMA((3, 2)),
                        pltpu.SemaphoreType.DMA((2,)),
                        pltpu.SemaphoreType.DMA((2,))],
    )
    return pl.pallas_call(
        functools.partial(_expert_mlp_kernel, li),
        grid_spec=grid_spec,
        out_shape=jax.ShapeDtypeStruct((n_rows, D_MODEL), F32),
        compiler_params=_params(1),
        name="expert_mlp",
    )(first, tiles_e, starts, slot_e, next_e, xs, w_gate, w_up, w_down)


def _combine_kernel(pos_ref, ys_ref, x_ref, gate_ref, g_ref, b_ref, of_ref, ob_ref, buf_ref, sem):
    base = pl.program_id(0) * ROW_BLK

    def tok(ti, carry):
        t = base + ti
        for k in range(2):
            pltpu.make_async_copy(ys_ref.at[pl.ds(pos_ref[2 * t + k], 1)],
                                  buf_ref.at[k, pl.ds(ti, 1)], sem).start()
        return carry

    lax.fori_loop(0, ROW_BLK, tok, 0, unroll=8)
    for k in range(2):
        pltpu.make_async_copy(ys_ref.at[pl.ds(0, ROW_BLK)], buf_ref.at[k], sem).wait()
    gates = gate_ref[...]
    y = ALPHA * x_ref[...] + gates[:, 0:1] * buf_ref[0] + gates[:, 1:2] * buf_ref[1]
    y = _layer_norm(y, g_ref[...], b_ref[...])
    of_ref[...] = y
    ob_ref[...] = y.astype(BF16)


def _combine_ln(pos_flat, ys, x, gates, g, b):
    t = x.shape[0]
    row = lambda i, pos: (i, 0)
    vec = lambda i, pos: (0, 0)
    grid_spec = pltpu.PrefetchScalarGridSpec(
        num_scalar_prefetch=1,
        grid=(t // ROW_BLK,),
        in_specs=[pl.BlockSpec(memory_space=pl.ANY),
                  pl.BlockSpec((ROW_BLK, D_MODEL), row),
                  pl.BlockSpec((ROW_BLK, 2), row),
                  pl.BlockSpec((1, D_MODEL), vec),
                  pl.BlockSpec((1, D_MODEL), vec)],
        out_specs=[pl.BlockSpec((ROW_BLK, D_MODEL), row),
                   pl.BlockSpec((ROW_BLK, D_MODEL), row)],
        scratch_shapes=[pltpu.VMEM((2, ROW_BLK, D_MODEL), F32), pltpu.SemaphoreType.DMA(())],
    )
    return pl.pallas_call(
        _combine_kernel,
        grid_spec=grid_spec,
        out_shape=[jax.ShapeDtypeStruct((t, D_MODEL), F32),
                   jax.ShapeDtypeStruct((t, D_MODEL), BF16)],
        compiler_params=_params(1),
        name="combine_ln",
    )(pos_flat, ys, x, gates, g.reshape(1, -1), b.reshape(1, -1))


def _moe_layer(x, w_pad, br_pad, w_gate, w_up, w_down, li, ln_g, ln_b):
    t = x.shape[0]
    n_tiles = -(-(2 * t + N_EXPERTS * (MOE_ROWS - 1)) // MOE_ROWS)
    ri, gates, cnt = _router(x, w_pad, br_pad)
    counts = cnt[0, :N_EXPERTS].astype(I32)
    tiles_e = (counts + MOE_ROWS - 1) // MOE_ROWS
    padded = tiles_e * MOE_ROWS
    starts = (jnp.cumsum(padded) - padded).astype(I32)
    onehot = ri[:, 0:2, None] == jnp.arange(N_EXPERTS, dtype=I32)
    pos = jnp.sum(jnp.where(onehot, starts, 0), axis=-1) + ri[:, 2:4]
    pos_flat = pos.reshape(-1)
    meta = jnp.concatenate([counts, starts, padded]).astype(I32)
    ids = jnp.arange(N_EXPERTS, dtype=I32)
    live = tiles_e > 0
    slot_e = ((jnp.cumsum(live) - live) % 2).astype(I32)
    later = live[None, :] & (ids[None, :] > ids[:, None])
    next_e = jnp.where(jnp.any(later, axis=1), jnp.argmax(later, axis=1), -1).astype(I32)
    first = jnp.argmax(live).astype(I32).reshape(1)

    xs = _dispatch(pos_flat, meta, x, n_tiles * MOE_ROWS)
    ys = _expert_mlp(first, tiles_e.astype(I32), starts, slot_e, next_e, xs, w_gate, w_up, w_down, li)
    return _combine_ln(pos_flat, ys, x, gates, ln_g, ln_b)


def kernel(x_prompt, x_sample, cache_k, cache_v, state_conv, w_in_even, ln_v_g, ln_v_b, w_spatial, b_spatial, sinks, w_out_even, w_in_odd, conv_w, w_out_odd, ln_mix_g, ln_mix_b, ln_ffn_g, ln_ffn_b, w_router, b_router, w_gate, w_up, w_down):
    n_batch, seq_p, _ = x_prompt.shape
    n_streams, seq_s, _ = x_sample.shape
    n_p = n_batch * seq_p
    n_s = n_streams * seq_s
    assert n_batch == 1 and n_s == ROW_BLK and n_p % ROW_BLK == 0 and (n_p + n_s) % MM_ROWS == 0
    assert seq_s & (seq_s - 1) == 0 and A_CHUNK % seq_s == 0 and cache_k.shape[2] == WINDOW
    n_prompt_blk = n_p // ROW_BLK
    win_p = min(WINDOW, seq_p)

    x = jnp.concatenate([x_prompt.reshape(n_p, D_MODEL), x_sample.reshape(n_s, D_MODEL)], axis=0)
    xb = x.astype(BF16)
    w_pad = jnp.pad(w_router, ((0, 0), (0, LANES - N_EXPERTS)))
    br_pad = jnp.pad(b_router, (0, LANES - N_EXPERTS)).reshape(1, LANES)
    reps = A_CHUNK // seq_s

    k_p, v_p, k_s, v_s, vg_s, c_p, c_s = [], [], [], [], [], [], []
    for layer in range(DEPTH):
        i = layer // 2
        if layer % 2 == 0:
            h3 = _matmul_cols(xb, w_in_even, i, 0, 6, 512, F32)
            kv = _matmul_cols(xb, w_in_even, i, 12, 1, 2 * KV_WIDTH, F32)
            w_stack = jnp.stack([w_spatial[i], jnp.tile(w_spatial[i][:, :seq_s, :seq_s], (1, reps, reps))])
            bt_stack = jnp.stack([b_spatial[i].T, jnp.tile(b_spatial[i][:, :seq_s], (1, reps)).T])
            mix, vn_s = _mixer_a(h3, w_stack, bt_stack, ln_v_g[i], ln_v_b[i], n_prompt_blk, seq_s)
            mix = _attn_prompt(h3, kv, mix, sinks[i], n_prompt_blk)
            mix = _attn_sample(h3, kv, mix, cache_k[i].reshape(n_streams * WINDOW, KV_WIDTH),
                               cache_v[i].reshape(n_streams * WINDOW, KV_WIDTH), sinks[i],
                               n_prompt_blk, n_streams, seq_s)
            x, xb = _out_proj_ln(mix, w_out_even, i, x, ln_mix_g[layer], ln_mix_b[layer])
            k_p.append(kv[n_p - win_p:n_p, :KV_WIDTH].reshape(n_batch, win_p, N_KV_HEADS, HEAD_DIM))
            v_p.append(kv[n_p - win_p:n_p, KV_WIDTH:].reshape(n_batch, win_p, N_KV_HEADS, HEAD_DIM))
            k_s.append(kv[n_p:, :KV_WIDTH].reshape(n_streams, seq_s, N_KV_HEADS, HEAD_DIM))
            v_s.append(kv[n_p:, KV_WIDTH:].reshape(n_streams, seq_s, N_KV_HEADS, HEAD_DIM))
            vg_s.append(vn_s.reshape(n_streams, seq_s, A_WIDTH))
        else:
            gate_out, z = _odd_in_proj(xb, w_in_odd, i)
            g = _gated_conv(z, gate_out, state_conv[i].reshape(n_streams * (CONV_WIDTH - 1), D_MODEL),
                            conv_w[i], n_prompt_blk, n_streams, seq_s)
            x, xb = _out_proj_ln(g, w_out_odd, i, x, ln_mix_g[layer], ln_mix_b[layer])
            c_p.append(z[n_p - (CONV_WIDTH - 1):n_p].reshape(n_batch, CONV_WIDTH - 1, D_MODEL))
            c_s.append(z[n_p:].reshape(n_streams, seq_s, D_MODEL)[:, seq_s - (CONV_WIDTH - 1):])
        x, xb = _moe_layer(x, w_pad, br_pad, w_gate, w_up, w_down, layer,
                           ln_ffn_g[layer], ln_ffn_b[layer])
    return (x[:n_p].reshape(n_batch, seq_p, D_MODEL), x[n_p:].reshape(n_streams, seq_s, D_MODEL),
            jnp.stack(k_p), jnp.stack(v_p), jnp.stack(k_s), jnp.stack(v_s),
            jnp.stack(vg_s), jnp.stack(c_p), jnp.stack(c_s))
```

```python
import functools

import jax
import jax.numpy as jnp
from jax import lax
from jax.experimental import pallas as pl
from jax.experimental.pallas import tpu as pltpu

F32 = jnp.float32
BF16 = jnp.bfloat16
I32 = jnp.int32

D_MODEL = 2048
DEPTH = 4
CHUNK = 64
A_HEADS = 8
A_HEAD_DIM = 128
A_WIDTH = A_HEADS * A_HEAD_DIM
A_CHUNK = 128
N_HEADS = 16
N_KV_HEADS = 2
HEAD_DIM = 64
Q_WIDTH = N_HEADS * HEAD_DIM
KV_WIDTH = N_KV_HEADS * HEAD_DIM
WINDOW = 128
CONV_WIDTH = 3
N_EXPERTS = 16
N_GROUPS = 4
EXPERTS_PER_GROUP = N_EXPERTS // N_GROUPS
D_EXPERT = 1024
ALPHA = (2 * DEPTH) ** 0.25
LN_EPS = 1e-5
NEG_INF = -1e30
ALIBI_SLOPES = tuple(2.0 ** (-8.0 * h / N_HEADS) for h in range(1, N_HEADS + 1))

LANES = 128
ROW_BLK = 256
MM_ROWS = 768
OUT_ROWS = 384
MOE_ROWS = 128
W_QUARTERS = 4
VMEM_LIMIT = 56 * 1024 * 1024


def _params(n_axes, **kw):
    return pltpu.CompilerParams(dimension_semantics=("arbitrary",) * n_axes,
                                vmem_limit_bytes=VMEM_LIMIT, **kw)


def _layer_norm(y, g, b):
    mu = jnp.mean(y, axis=-1, keepdims=True)
    yc = y - mu
    var = jnp.mean(yc * yc, axis=-1, keepdims=True)
    return yc * lax.rsqrt(var + LN_EPS) * g + b


def _odd_in_kernel(x_ref, wo_ref, wi_ref, wh_ref, go_ref, z_ref, wb_ref):
    @pl.when(pl.program_id(1) == 0)
    def _():
        wb_ref[0] = wo_ref[0].astype(BF16)
        wb_ref[1] = wi_ref[0].astype(BF16)
        wb_ref[2] = wh_ref[0].astype(BF16)

    x = x_ref[...]
    go_ref[...] = jnp.dot(x, wb_ref[0], preferred_element_type=F32)
    gate_in = jnp.dot(x, wb_ref[1], preferred_element_type=F32)
    h = jnp.dot(x, wb_ref[2], preferred_element_type=F32)
    z_ref[...] = gate_in * h


def _odd_in_proj(xb, w_in, li):
    t, k = xb.shape
    tn = 256
    nt = D_MODEL // tn
    out = jax.ShapeDtypeStruct((t, D_MODEL), F32)
    return pl.pallas_call(
        _odd_in_kernel,
        grid=(nt, t // MM_ROWS),
        in_specs=[pl.BlockSpec((MM_ROWS, k), lambda j, i: (i, 0)),
                  pl.BlockSpec((1, k, tn), lambda j, i: (li, 0, j)),
                  pl.BlockSpec((1, k, tn), lambda j, i: (li, 0, nt + j)),
                  pl.BlockSpec((1, k, tn), lambda j, i: (li, 0, 2 * nt + j))],
        out_specs=[pl.BlockSpec((MM_ROWS, tn), lambda j, i: (i, j)),
                   pl.BlockSpec((MM_ROWS, tn), lambda j, i: (i, j))],
        out_shape=[out, out],
        scratch_shapes=[pltpu.VMEM((3, k, tn), BF16)],
        compiler_params=_params(2),
        name="odd_in_proj",
    )(xb, w_in, w_in, w_in)


def _load_weight_bf16(w_ref, li, wb_ref, stage_ref, sem):
    kdim = wb_ref.shape[0]
    rows = stage_ref.shape[1]
    n_chunks = kdim // rows

    @pl.when(pl.program_id(0) == 0)
    def _():
        def chunk_copy(c):
            return pltpu.make_async_copy(w_ref.at[li, pl.ds(c * rows, rows)], stage_ref.at[c % 2], sem.at[c % 2])

        chunk_copy(0).start()
        for c in range(n_chunks):
            if c + 1 < n_chunks:
                chunk_copy(c + 1).start()
            chunk_copy(c).wait()
            wb_ref[c * rows:(c + 1) * rows, :] = stage_ref[c % 2].astype(BF16)


def _even_in_kernel(li, x_ref, w_ref, h_ref, kv_ref, wb_ref, stage_ref, sem):
    _load_weight_bf16(w_ref, li, wb_ref, stage_ref, sem)
    out = jnp.dot(x_ref[...], wb_ref[...], preferred_element_type=F32)
    n_h = h_ref.shape[1]
    h_ref[...] = out[:, :n_h]
    kv_ref[...] = out[:, n_h:]


def _even_in_proj(xb, w_in, li):
    t, k = xb.shape
    n_all = w_in.shape[2]
    n_kv = 2 * KV_WIDTH
    row = lambda i: (i, 0)
    return pl.pallas_call(
        functools.partial(_even_in_kernel, li),
        grid=(t // OUT_ROWS,),
        in_specs=[pl.BlockSpec((OUT_ROWS, k), row), pl.BlockSpec(memory_space=pl.ANY)],
        out_specs=[pl.BlockSpec((OUT_ROWS, n_all - n_kv), row), pl.BlockSpec((OUT_ROWS, n_kv), row)],
        out_shape=[jax.ShapeDtypeStruct((t, n_all - n_kv), F32), jax.ShapeDtypeStruct((t, n_kv), F32)],
        scratch_shapes=[pltpu.VMEM((k, n_all), BF16),
                        pltpu.VMEM((2, k // 8, n_all), F32),
                        pltpu.SemaphoreType.DMA((2,))],
        compiler_params=_params(1),
        name="even_in_proj",
    )(xb, w_in)


def _out_proj_kernel(li, l_ref, w_ref, r_ref, g_ref, b_ref, of_ref, ob_ref, wb_ref, stage_ref, sem):
    _load_weight_bf16(w_ref, li, wb_ref, stage_ref, sem)
    y = ALPHA * r_ref[...] + jnp.dot(l_ref[...], wb_ref[...], preferred_element_type=F32)
    y = _layer_norm(y, g_ref[...], b_ref[...])
    of_ref[...] = y
    ob_ref[...] = y.astype(BF16)


def _out_proj_ln(lhs, w, li, resid, g, b):
    t, kdim = lhs.shape
    row = lambda i: (i, 0)
    vec = lambda i: (0, 0)
    return pl.pallas_call(
        functools.partial(_out_proj_kernel, li),
        grid=(t // OUT_ROWS,),
        in_specs=[pl.BlockSpec((OUT_ROWS, kdim), row),
                  pl.BlockSpec(memory_space=pl.ANY),
                  pl.BlockSpec((OUT_ROWS, D_MODEL), row),
                  pl.BlockSpec((1, D_MODEL), vec),
                  pl.BlockSpec((1, D_MODEL), vec)],
        out_specs=[pl.BlockSpec((OUT_ROWS, D_MODEL), row),
                   pl.BlockSpec((OUT_ROWS, D_MODEL), row)],
        out_shape=[jax.ShapeDtypeStruct((t, D_MODEL), F32),
                   jax.ShapeDtypeStruct((t, D_MODEL), BF16)],
        scratch_shapes=[pltpu.VMEM((kdim, D_MODEL), BF16),
                        pltpu.VMEM((2, kdim // 4, D_MODEL), F32),
                        pltpu.SemaphoreType.DMA((2,))],
        compiler_params=_params(1),
        name="out_proj_ln",
    )(lhs, w, resid, g.reshape(1, -1), b.reshape(1, -1))


def _mixer_a_kernel(n_prompt_blk, seq_s, u_ref, v_ref, w_ref, bt_ref, g_ref, b_ref, a_ref, vn_ref):
    i = pl.program_id(0)
    is_sample = i == n_prompt_blk
    vn = _layer_norm(v_ref[...], g_ref[...], b_ref[...])

    @pl.when(is_sample)
    def _():
        vn_ref[...] = vn

    shift = jnp.where(is_sample, seq_s.bit_length() - 1, A_CHUNK.bit_length() - 1)
    low = jnp.where(is_sample, seq_s - 1, A_CHUNK - 1)
    r = lax.broadcasted_iota(I32, (A_CHUNK, A_CHUNK), 0)
    c = lax.broadcasted_iota(I32, (A_CHUNK, A_CHUNK), 1)
    mask = ((r >> shift) == (c >> shift)) & ((c & low) <= (r & low))
    vb = vn.astype(BF16)
    bt = bt_ref[0]
    for h in range(A_HEADS):
        wm = jnp.where(mask, w_ref[0, h], 0.0).astype(BF16)
        bcol = bt[:, h:h + 1]
        cs = slice(h * A_HEAD_DIM, (h + 1) * A_HEAD_DIM)
        for n in range(ROW_BLK // A_CHUNK):
            rs = slice(n * A_CHUNK, (n + 1) * A_CHUNK)
            s = jnp.dot(wm, vb[rs, cs], preferred_element_type=F32) + bcol
            a_ref[rs, cs] = (u_ref[rs, cs] * s).astype(BF16)


def _mixer_a(h3, w_stack, bt_stack, ln_g, ln_b, n_prompt_blk, seq_s):
    t = h3.shape[0]
    return pl.pallas_call(
        functools.partial(_mixer_a_kernel, n_prompt_blk, seq_s),
        grid=(t // ROW_BLK,),
        in_specs=[pl.BlockSpec((ROW_BLK, A_WIDTH), lambda i: (i, 0)),
                  pl.BlockSpec((ROW_BLK, A_WIDTH), lambda i: (i, 1)),
                  pl.BlockSpec((1, A_HEADS, A_CHUNK, A_CHUNK), lambda i: (i // n_prompt_blk, 0, 0, 0)),
                  pl.BlockSpec((1, A_CHUNK, A_HEADS), lambda i: (i // n_prompt_blk, 0, 0)),
                  pl.BlockSpec((1, A_WIDTH), lambda i: (0, 0)),
                  pl.BlockSpec((1, A_WIDTH), lambda i: (0, 0))],
        out_specs=[pl.BlockSpec((ROW_BLK, A_WIDTH), lambda i: (i, 0)),
                   pl.BlockSpec((ROW_BLK, A_WIDTH), lambda i: (0, 0))],
        out_shape=[jax.ShapeDtypeStruct((t, 2 * A_WIDTH), BF16),
                   jax.ShapeDtypeStruct((ROW_BLK, A_WIDTH), F32)],
        compiler_params=_params(1),
        name="mixer_a",
    )(h3, h3, w_stack, bt_stack, ln_g.reshape(1, -1), ln_b.reshape(1, -1))


def _attention_heads(q, k, v, dist, mask, sinks_ref, o_ref):
    grp = N_HEADS // N_KV_HEADS
    for kv in range(N_KV_HEADS):
        kg = k[:, kv * HEAD_DIM:(kv + 1) * HEAD_DIM]
        vg = v[:, kv * HEAD_DIM:(kv + 1) * HEAD_DIM]
        outs = []
        for j in range(grp):
            h = kv * grp + j
            qh = q[:, h * HEAD_DIM:(h + 1) * HEAD_DIM]
            s = lax.dot_general(qh, kg, (((1,), (1,)), ((), ())), preferred_element_type=F32)
            s = jnp.where(mask, s * (HEAD_DIM ** -0.5) - ALIBI_SLOPES[h] * dist, NEG_INF)
            sink = sinks_ref[h]
            m = jnp.maximum(jnp.max(s, axis=-1, keepdims=True), sink)
            p = jnp.exp(s - m)
            den = jnp.sum(p, axis=-1, keepdims=True) + jnp.exp(sink - m)
            outs.append(jnp.dot(p.astype(BF16), vg, preferred_element_type=F32) / den)
        for j in range(0, grp, 2):
            c0 = (kv * grp + j) * HEAD_DIM
            o_ref[:, c0:c0 + 2 * HEAD_DIM] = jnp.concatenate(outs[j:j + 2], axis=1).astype(BF16)


def _attn_prompt_kernel(sinks_ref, q_ref, kv_ref, halo_ref, mix_ref, o_ref):
    del mix_ref
    i = pl.program_id(0)
    kv_all = jnp.concatenate([halo_ref[...], kv_ref[...]], axis=0).astype(BF16)
    n_keys = WINDOW + ROW_BLK
    r = lax.broadcasted_iota(I32, (ROW_BLK, n_keys), 0)
    c = lax.broadcasted_iota(I32, (ROW_BLK, n_keys), 1)
    dist = jnp.abs(r + WINDOW - c).astype(F32)
    qc = r // CHUNK
    kc = c // CHUNK
    mask = (kc >= qc) & (kc <= qc + WINDOW // CHUNK) & (c >= jnp.where(i > 0, 0, WINDOW))
    _attention_heads(q_ref[...].astype(BF16), kv_all[:, :KV_WIDTH], kv_all[:, KV_WIDTH:],
                     dist, mask, sinks_ref, o_ref)


def _attn_prompt(h3, kv, mix, sinks, n_prompt_blk):
    t = h3.shape[0]
    halo_per_blk = ROW_BLK // WINDOW
    return pl.pallas_call(
        _attn_prompt_kernel,
        grid=(n_prompt_blk,),
        in_specs=[pl.BlockSpec(memory_space=pltpu.SMEM),
                  pl.BlockSpec((ROW_BLK, Q_WIDTH), lambda i: (i, 2)),
                  pl.BlockSpec((ROW_BLK, 2 * KV_WIDTH), lambda i: (i, 0)),
                  pl.BlockSpec((WINDOW, 2 * KV_WIDTH), lambda i: (jnp.maximum(i * halo_per_blk - 1, 0), 0)),
                  pl.BlockSpec(memory_space=pl.ANY)],
        out_specs=pl.BlockSpec((ROW_BLK, Q_WIDTH), lambda i: (i, 1)),
        out_shape=jax.ShapeDtypeStruct((t, 2 * Q_WIDTH), BF16),
        input_output_aliases={4: 0},
        compiler_params=_params(1),
        name="attn_prompt",
    )(sinks, h3, kv, kv, mix)


def _attn_sample_kernel(n_streams, seq_s, sinks_ref, q_ref, kv_ref, ck_ref, cv_ref, mix_ref, o_ref):
    del mix_ref
    n_cache = ck_ref.shape[0]
    w_c = n_cache // n_streams
    k_all = jnp.concatenate([ck_ref[...], kv_ref[:, :KV_WIDTH]], axis=0).astype(BF16)
    v_all = jnp.concatenate([cv_ref[...], kv_ref[:, KV_WIDTH:]], axis=0).astype(BF16)
    n_keys = n_cache + ROW_BLK
    r = lax.broadcasted_iota(I32, (ROW_BLK, n_keys), 0)
    c = lax.broadcasted_iota(I32, (ROW_BLK, n_keys), 1)
    is_new = c >= n_cache
    k_stream = jnp.where(is_new, (c - n_cache) // seq_s, c // w_c)
    k_pos = jnp.where(is_new, w_c + (c - n_cache) % seq_s, c % w_c)
    dist = jnp.abs(w_c + r % seq_s - k_pos).astype(F32)
    mask = k_stream == r // seq_s
    _attention_heads(q_ref[...].astype(BF16), k_all, v_all, dist, mask, sinks_ref, o_ref)


def _attn_sample(h3, kv, mix, cache_k2d, cache_v2d, sinks, n_prompt_blk, n_streams, seq_s):
    t = h3.shape[0]
    full = lambda i: (0, 0)
    return pl.pallas_call(
        functools.partial(_attn_sample_kernel, n_streams, seq_s),
        grid=(1,),
        in_specs=[pl.BlockSpec(memory_space=pltpu.SMEM),
                  pl.BlockSpec((ROW_BLK, Q_WIDTH), lambda i: (n_prompt_blk, 2)),
                  pl.BlockSpec((ROW_BLK, 2 * KV_WIDTH), lambda i: (n_prompt_blk, 0)),
                  pl.BlockSpec(cache_k2d.shape, full),
                  pl.BlockSpec(cache_v2d.shape, full),
                  pl.BlockSpec(memory_space=pl.ANY)],
        out_specs=pl.BlockSpec((ROW_BLK, Q_WIDTH), lambda i: (n_prompt_blk, 1)),
        out_shape=jax.ShapeDtypeStruct((t, 2 * Q_WIDTH), BF16),
        input_output_aliases={5: 0},
        compiler_params=_params(1),
        name="attn_sample",
    )(sinks, h3, kv, cache_k2d, cache_v2d, mix)


def _conv_kernel(n_prompt_blk, n_streams, seq_s, z_ref, halo_ref, go_ref, st_ref, cw_ref, o_ref):
    i = pl.program_id(0)
    z = z_ref[...]
    rows = lax.broadcasted_iota(I32, (ROW_BLK, 1), 0)
    zm1 = pltpu.roll(z, 1, 0)
    zm2 = pltpu.roll(z, 2, 0)

    def finish(zm1, zm2):
        conv = cw_ref[0:1, :] * zm2 + cw_ref[1:2, :] * zm1 + cw_ref[2:3, :] * z
        o_ref[...] = (go_ref[...] * conv).astype(BF16)

    @pl.when(i < n_prompt_blk)
    def _():
        live = jnp.where(i > 0, 1.0, 0.0)
        p1 = halo_ref[7:8, :] * live
        p2 = halo_ref[6:7, :] * live
        finish(jnp.where(rows == 0, p1, zm1),
               jnp.where(rows == 0, p2, jnp.where(rows == 1, p1, zm2)))

    @pl.when(i == n_prompt_blk)
    def _():
        a, b = zm1, zm2
        for s in range(n_streams):
            s0 = st_ref[2 * s:2 * s + 1, :]
            s1 = st_ref[2 * s + 1:2 * s + 2, :]
            a = jnp.where(rows == s * seq_s, s1, a)
            b = jnp.where(rows == s * seq_s, s0, jnp.where(rows == s * seq_s + 1, s1, b))
        finish(a, b)


def _gated_conv(z, gate_out, state2d, conv_w, n_prompt_blk, n_streams, seq_s):
    t = z.shape[0]
    halo_per_blk = ROW_BLK // 8
    return pl.pallas_call(
        functools.partial(_conv_kernel, n_prompt_blk, n_streams, seq_s),
        grid=(t // ROW_BLK,),
        in_specs=[pl.BlockSpec((ROW_BLK, D_MODEL), lambda i: (i, 0)),
                  pl.BlockSpec((8, D_MODEL), lambda i: (jnp.maximum(i * halo_per_blk - 1, 0), 0)),
                  pl.BlockSpec((ROW_BLK, D_MODEL), lambda i: (i, 0)),
                  pl.BlockSpec(state2d.shape, lambda i: (0, 0)),
                  pl.BlockSpec(conv_w.shape, lambda i: (0, 0))],
        out_specs=pl.BlockSpec((ROW_BLK, D_MODEL), lambda i: (i, 0)),
        out_shape=jax.ShapeDtypeStruct((t, D_MODEL), BF16),
        compiler_params=_params(1),
        name="gated_conv",
    )(z, z, gate_out, state2d, conv_w)


def _split_bf16(x):
    hi = x.astype(BF16)
    lo = (x - hi.astype(F32)).astype(BF16)
    return hi, lo


def _router_kernel(n_blk, x_ref, w_ref, br_ref, ri_ref, gate_ref, cnt_ref, carry_ref, wsplit_ref):
    i = pl.program_id(0)

    @pl.when(i == 0)
    def _():
        carry_ref[...] = jnp.zeros_like(carry_ref)
        wh, wl = _split_bf16(w_ref[...])
        wsplit_ref[0] = wh
        wsplit_ref[1] = wl

    xh, xl = _split_bf16(x_ref[...])
    wh, wl = wsplit_ref[0], wsplit_ref[1]
    logits = (jnp.dot(xh, wh, preferred_element_type=F32) + jnp.dot(xh, wl, preferred_element_type=F32)
              + jnp.dot(xl, wh, preferred_element_type=F32))
    lt = logits.T[:N_EXPERTS]
    ex = jnp.exp(lt - jnp.max(lt, axis=0, keepdims=True))
    scores = ex / jnp.sum(ex, axis=0, keepdims=True)
    biased = scores + br_ref[...]
    sc = [scores[e:e + 1] for e in range(N_EXPERTS)]
    bs = [biased[e:e + 1] for e in range(N_EXPERTS)]

    gscore = []
    for g in range(N_GROUPS):
        a, b, c, d = bs[4 * g:4 * g + 4]
        hi1, lo1 = jnp.maximum(a, b), jnp.minimum(a, b)
        hi2, lo2 = jnp.maximum(c, d), jnp.minimum(c, d)
        gscore.append(jnp.maximum(hi1, hi2) + jnp.maximum(jnp.minimum(hi1, hi2), jnp.maximum(lo1, lo2)))
    best, grp = gscore[0], jnp.zeros((1, ROW_BLK), I32)
    for g in range(1, N_GROUPS):
        better = gscore[g] > best
        grp = jnp.where(better, g, grp)
        best = jnp.where(better, gscore[g], best)

    def pick(rows, idx, n):
        out = rows[0]
        for j in range(1, n):
            out = jnp.where(idx == j, rows[j], out)
        return out

    in_b = [pick([bs[4 * g + j] for g in range(N_GROUPS)], grp, N_GROUPS) for j in range(EXPERTS_PER_GROUP)]
    in_s = [pick([sc[4 * g + j] for g in range(N_GROUPS)], grp, N_GROUPS) for j in range(EXPERTS_PER_GROUP)]
    v1, l1 = in_b[0], jnp.zeros((1, ROW_BLK), I32)
    for j in range(1, EXPERTS_PER_GROUP):
        better = in_b[j] > v1
        l1 = jnp.where(better, j, l1)
        v1 = jnp.where(better, in_b[j], v1)
    v2, l2 = jnp.full((1, ROW_BLK), -jnp.inf, F32), jnp.zeros((1, ROW_BLK), I32)
    for j in range(EXPERTS_PER_GROUP):
        better = (l1 != j) & (in_b[j] > v2)
        l2 = jnp.where(better, j, l2)
        v2 = jnp.where(better, in_b[j], v2)
    gate1 = pick(in_s, l1, EXPERTS_PER_GROUP)
    gate2 = pick(in_s, l2, EXPERTS_PER_GROUP)
    gsum = gate1 + gate2
    gate_ref[0:1, :] = gate1 / gsum
    gate_ref[1:2, :] = gate2 / gsum
    e1 = grp * EXPERTS_PER_GROUP + l1
    e2 = grp * EXPERTS_PER_GROUP + l2
    ri_ref[0:1, :] = e1
    ri_ref[1:2, :] = e2

    expert = lax.broadcasted_iota(I32, (N_EXPERTS, ROW_BLK), 0)
    oh1 = (expert == e1).astype(F32)
    oh2 = (expert == e2).astype(F32)
    r = lax.broadcasted_iota(I32, (ROW_BLK, ROW_BLK), 0)
    c = lax.broadcasted_iota(I32, (ROW_BLK, ROW_BLK), 1)
    before = (r < c).astype(BF16)
    cs = jnp.dot(jnp.concatenate([oh1, oh2], axis=0).astype(BF16), before, preferred_element_type=F32)
    tot1 = jnp.sum(oh1, axis=1, keepdims=True)
    tot2 = jnp.sum(oh2, axis=1, keepdims=True)
    carry = carry_ref[...][:, 0:1]
    ri_ref[2:3, :] = jnp.sum(oh1 * (carry + cs[:N_EXPERTS]), axis=0, keepdims=True).astype(I32)
    ri_ref[3:4, :] = jnp.sum(oh2 * (carry + tot1 + cs[N_EXPERTS:]), axis=0, keepdims=True).astype(I32)
    carry_ref[...] = carry_ref[...] + tot1 + tot2

    @pl.when(i == n_blk - 1)
    def _():
        cnt_ref[...] = carry_ref[...]


def _router(x, w_pad, b_col):
    t = x.shape[0]
    n_blk = t // ROW_BLK
    return pl.pallas_call(
        functools.partial(_router_kernel, n_blk),
        grid=(n_blk,),
        in_specs=[pl.BlockSpec((ROW_BLK, D_MODEL), lambda i: (i, 0)),
                  pl.BlockSpec((D_MODEL, LANES), lambda i: (0, 0)),
                  pl.BlockSpec((N_EXPERTS, 1), lambda i: (0, 0))],
        out_specs=[pl.BlockSpec((4, ROW_BLK), lambda i: (0, i)),
                   pl.BlockSpec((2, ROW_BLK), lambda i: (0, i)),
                   pl.BlockSpec((N_EXPERTS, LANES), lambda i: (0, 0))],
        out_shape=[jax.ShapeDtypeStruct((4, t), I32),
                   jax.ShapeDtypeStruct((2, t), F32),
                   jax.ShapeDtypeStruct((N_EXPERTS, LANES), F32)],
        scratch_shapes=[pltpu.VMEM((N_EXPERTS, LANES), F32), pltpu.VMEM((2, D_MODEL, LANES), BF16)],
        compiler_params=_params(1),
        name="router",
    )(x, w_pad, b_col)


def _dispatch_kernel(pos_ref, meta_ref, x_ref, xs_ref, sem):
    i = pl.program_id(0)
    base = i * ROW_BLK

    def row_copy(src_row, dst_row):
        return pltpu.make_async_copy(x_ref.at[pl.ds(src_row, 1)], xs_ref.at[pl.ds(dst_row, 1)], sem)

    def tok(ti, carry):
        row_copy(ti, pos_ref[2 * (base + ti)]).start()
        row_copy(ti, pos_ref[2 * (base + ti) + 1]).start()
        return carry

    lax.fori_loop(0, ROW_BLK, tok, 0, unroll=8)

    @pl.when(i == 0)
    def _():
        for e in range(N_EXPERTS):
            cnt, start, padded = meta_ref[e], meta_ref[N_EXPERTS + e], meta_ref[2 * N_EXPERTS + e]

            def fill(ri, carry):
                row_copy(0, start + ri).start()
                return carry

            def drain(ri, carry):
                row_copy(0, 0).wait()
                return carry

            lax.fori_loop(cnt, padded, fill, 0)
            lax.fori_loop(cnt, padded, drain, 0)

    for _ in range(2):
        pltpu.make_async_copy(x_ref, xs_ref.at[pl.ds(0, ROW_BLK)], sem).wait()


def _dispatch(pos_flat, meta, x, n_rows):
    t = x.shape[0]
    grid_spec = pltpu.PrefetchScalarGridSpec(
        num_scalar_prefetch=2,
        grid=(t // ROW_BLK,),
        in_specs=[pl.BlockSpec((ROW_BLK, D_MODEL), lambda i, pos, meta: (i, 0))],
        out_specs=pl.BlockSpec(memory_space=pl.ANY),
        scratch_shapes=[pltpu.SemaphoreType.DMA(())],
    )
    return pl.pallas_call(
        _dispatch_kernel,
        grid_spec=grid_spec,
        out_shape=jax.ShapeDtypeStruct((n_rows, D_MODEL), x.dtype),
        compiler_params=_params(1),
        name="dispatch",
    )(pos_flat, meta, x)


def _expert_mlp_kernel(li, first_ref, nt_ref, row0_ref, slot_ref, next_ref, xs_ref, wg_ref, wu_ref, wd_ref,
                       ys_ref, wgb, wub, wdb, sg, su, sd, xstage, ostage, w_sem, x_sem, y_sem):
    e = pl.program_id(0)
    nt = nt_ref[e]
    row0 = row0_ref[e]
    slot = slot_ref[e]
    e_next = next_ref[e]
    prefetch = e_next >= 0
    w_parts = ((wg_ref, sg, wgb, D_MODEL // W_QUARTERS),
               (wu_ref, su, wub, D_MODEL // W_QUARTERS),
               (wd_ref, sd, wdb, D_EXPERT // W_QUARTERS))

    def w_copy(k, expert, q, s):
        src, stage, _, rows = w_parts[k]
        return pltpu.make_async_copy(src.at[li, expert, pl.ds(pl.multiple_of(q * rows, rows), rows)],
                                     stage.at[s], w_sem.at[k, s])

    def w_quarter(expert, q, dst_slot):
        s = q % 2
        for k in range(3):
            w_copy(k, expert, q, s).wait()

        @pl.when(q + 1 < W_QUARTERS)
        def _():
            for k in range(3):
                w_copy(k, expert, q + 1, 1 - s).start()

        for _, stage, dst, rows in w_parts:
            dst[dst_slot, pl.ds(pl.multiple_of(q * rows, rows), rows), :] = stage[s].astype(BF16)

    def hbm_rows(r):
        return pl.ds(pl.multiple_of(row0 + r * MOE_ROWS, MOE_ROWS), MOE_ROWS)

    def x_load(r, s):
        return pltpu.make_async_copy(xs_ref.at[hbm_rows(r)], xstage.at[s], x_sem.at[s])

    def y_store(r, s):
        return pltpu.make_async_copy(ostage.at[s], ys_ref.at[hbm_rows(r)], y_sem.at[s])

    def tile(r):
        s = r % 2
        x_load(r, s).wait()

        @pl.when(r + 1 < nt)
        def _():
            x_load(r + 1, 1 - s).start()

        xb = xstage[s].astype(BF16)
        g = jnp.dot(xb, wgb[slot], preferred_element_type=F32)
        u = jnp.dot(xb, wub[slot], preferred_element_type=F32)
        h = (g * jax.nn.sigmoid(g) * u).astype(BF16)
        out = jnp.dot(h, wdb[slot], preferred_element_type=F32)

        @pl.when(r >= 2)
        def _():
            y_store(r - 2, s).wait()

        ostage[s] = out
        y_store(r, s).start()

    @pl.when(e == first_ref[0])
    def _():
        for k in range(3):
            w_copy(k, e, 0, 0).start()

        def own(q, carry):
            w_quarter(e, q, slot)
            return carry

        lax.fori_loop(0, W_QUARTERS, own, 0)

    @pl.when(nt > 0)
    def _():
        @pl.when(prefetch)
        def _():
            for k in range(3):
                w_copy(k, e_next, 0, 0).start()

        x_load(0, 0).start()

        def step(r, carry):
            @pl.when(r < nt)
            def _():
                tile(r)

            @pl.when(prefetch & (r < W_QUARTERS))
            def _():
                w_quarter(e_next, r, 1 - slot)

            return carry

        lax.fori_loop(0, jnp.maximum(nt, jnp.where(prefetch, W_QUARTERS, 0)), step, 0)
        y_store(0, (nt - 1) % 2).wait()

        @pl.when(nt >= 2)
        def _():
            y_store(0, nt % 2).wait()


def _expert_mlp(first, tiles_e, starts, slot_e, next_e, xs, w_gate, w_up, w_down, li):
    n_rows = xs.shape[0]
    qg, qd = D_MODEL // W_QUARTERS, D_EXPERT // W_QUARTERS
    grid_spec = pltpu.PrefetchScalarGridSpec(
        num_scalar_prefetch=5,
        grid=(N_EXPERTS,),
        in_specs=[pl.BlockSpec(memory_space=pl.ANY)] * 4,
        out_specs=pl.BlockSpec(memory_space=pl.ANY),
        scratch_shapes=[pltpu.VMEM((2, D_MODEL, D_EXPERT), BF16),
                        pltpu.VMEM((2, D_MODEL, D_EXPERT), BF16),
                        pltpu.VMEM((2, D_EXPERT, D_MODEL), BF16),
                        pltpu.VMEM((2, qg, D_EXPERT), F32),
                        pltpu.VMEM((2, qg, D_EXPERT), F32),
                        pltpu.VMEM((2, qd, D_MODEL), F32),
                        pltpu.VMEM((2, MOE_ROWS, D_MODEL), xs.dtype),
                        pltpu.VMEM((2, MOE_ROWS, D_MODEL), F32),
                        pltpu.SemaphoreType.DMA((3, 2)),
                        pltpu.SemaphoreType.DMA((2,)),
                        pltpu.SemaphoreType.DMA((2,))],
    )
    return pl.pallas_call(
        functools.partial(_expert_mlp_kernel, li),
        grid_spec=grid_spec,
        out_shape=jax.ShapeDtypeStruct((n_rows, D_MODEL), F32),
        compiler_params=_params(1),
        name="expert_mlp",
    )(first, tiles_e, starts, slot_e, next_e, xs, w_gate, w_up, w_down)


def _combine_kernel(n_prompt_blk, pos_ref, ys_ref, x_ref, gate_ref, g_ref, b_ref, oa_ref, ob_ref, buf_ref, sem):
    i = pl.program_id(0)
    slot = i % 2

    def gather(blk, s):
        def tok(ti, carry):
            t = blk * ROW_BLK + ti
            for k in range(2):
                pltpu.make_async_copy(ys_ref.at[pl.ds(pos_ref[2 * t + k], 1)],
                                      buf_ref.at[s, k, pl.ds(ti, 1)], sem.at[s]).start()
            return carry

        lax.fori_loop(0, ROW_BLK, tok, 0, unroll=8)

    @pl.when(i == 0)
    def _():
        gather(0, 0)

    @pl.when(i + 1 < pl.num_programs(0))
    def _():
        gather(i + 1, 1 - slot)

    for k in range(2):
        pltpu.make_async_copy(ys_ref.at[pl.ds(0, ROW_BLK)], buf_ref.at[slot, k], sem.at[slot]).wait()
    gates = gate_ref[...]
    y = ALPHA * x_ref[...] + gates[:, 0:1] * buf_ref[slot, 0] + gates[:, 1:2] * buf_ref[slot, 1]
    y = _layer_norm(y, g_ref[...], b_ref[...])
    if n_prompt_blk is None:
        oa_ref[...] = y
        ob_ref[...] = y.astype(BF16)
    else:
        @pl.when(i < n_prompt_blk)
        def _():
            oa_ref[...] = y

        @pl.when(i == n_prompt_blk)
        def _():
            ob_ref[...] = y


def _combine_ln(pos_flat, ys, x, gates, g, b, n_prompt_blk=None):
    t = x.shape[0]
    row = lambda i, pos: (i, 0)
    vec = lambda i, pos: (0, 0)
    if n_prompt_blk is None:
        out_specs = [pl.BlockSpec((ROW_BLK, D_MODEL), row), pl.BlockSpec((ROW_BLK, D_MODEL), row)]
        out_shape = [jax.ShapeDtypeStruct((t, D_MODEL), F32), jax.ShapeDtypeStruct((t, D_MODEL), BF16)]
    else:
        out_specs = [pl.BlockSpec((ROW_BLK, D_MODEL), lambda i, pos: (jnp.minimum(i, n_prompt_blk - 1), 0)),
                     pl.BlockSpec((ROW_BLK, D_MODEL), vec)]
        out_shape = [jax.ShapeDtypeStruct((n_prompt_blk * ROW_BLK, D_MODEL), F32),
                     jax.ShapeDtypeStruct((t - n_prompt_blk * ROW_BLK, D_MODEL), F32)]
    grid_spec = pltpu.PrefetchScalarGridSpec(
        num_scalar_prefetch=1,
        grid=(t // ROW_BLK,),
        in_specs=[pl.BlockSpec(memory_space=pl.ANY),
                  pl.BlockSpec((ROW_BLK, D_MODEL), row),
                  pl.BlockSpec((ROW_BLK, 2), row),
                  pl.BlockSpec((1, D_MODEL), vec),
                  pl.BlockSpec((1, D_MODEL), vec)],
        out_specs=out_specs,
        scratch_shapes=[pltpu.VMEM((2, 2, ROW_BLK, D_MODEL), F32), pltpu.SemaphoreType.DMA((2,))],
    )
    return pl.pallas_call(
        functools.partial(_combine_kernel, n_prompt_blk),
        grid_spec=grid_spec,
        out_shape=out_shape,
        compiler_params=_params(1),
        name="combine_ln",
    )(pos_flat, ys, x, gates, g.reshape(1, -1), b.reshape(1, -1))


def _moe_layer(x, w_pad, br_pad, w_gate, w_up, w_down, li, ln_g, ln_b, n_prompt_blk=None):
    t = x.shape[0]
    n_tiles = -(-(2 * t + N_EXPERTS * (MOE_ROWS - 1)) // MOE_ROWS)
    ri, gates, cnt = _router(x, w_pad, br_pad)
    gates = gates.T
    counts = cnt[:, 0].astype(I32)
    tiles_e = (counts + MOE_ROWS - 1) // MOE_ROWS
    padded = tiles_e * MOE_ROWS
    starts = (jnp.cumsum(padded) - padded).astype(I32)
    onehot = ri[0:2, :, None] == jnp.arange(N_EXPERTS, dtype=I32)
    pos = jnp.sum(jnp.where(onehot, starts, 0), axis=-1) + ri[2:4]
    pos_flat = pos.T.reshape(-1)
    meta = jnp.concatenate([counts, starts, padded]).astype(I32)
    ids = jnp.arange(N_EXPERTS, dtype=I32)
    live = tiles_e > 0
    slot_e = ((jnp.cumsum(live) - live) % 2).astype(I32)
    later = live[None, :] & (ids[None, :] > ids[:, None])
    next_e = jnp.where(jnp.any(later, axis=1), jnp.argmax(later, axis=1), -1).astype(I32)
    first = jnp.argmax(live).astype(I32).reshape(1)

    xs = _dispatch(pos_flat, meta, x, n_tiles * MOE_ROWS)
    ys = _expert_mlp(first, tiles_e.astype(I32), starts, slot_e, next_e, xs, w_gate, w_up, w_down, li)
    return _combine_ln(pos_flat, ys, x, gates, ln_g, ln_b, n_prompt_blk)


def kernel(x_prompt, x_sample, cache_k, cache_v, state_conv, w_in_even, ln_v_g, ln_v_b, w_spatial, b_spatial, sinks, w_out_even, w_in_odd, conv_w, w_out_odd, ln_mix_g, ln_mix_b, ln_ffn_g, ln_ffn_b, w_router, b_router, w_gate, w_up, w_down):
    n_batch, seq_p, _ = x_prompt.shape
    n_streams, seq_s, _ = x_sample.shape
    n_p = n_batch * seq_p
    n_s = n_streams * seq_s
    assert n_batch == 1 and n_s == ROW_BLK and n_p % ROW_BLK == 0 and (n_p + n_s) % MM_ROWS == 0
    assert seq_s & (seq_s - 1) == 0 and A_CHUNK % seq_s == 0 and cache_k.shape[2] == WINDOW
    n_prompt_blk = n_p // ROW_BLK
    win_p = min(WINDOW, seq_p)

    x = jnp.concatenate([x_prompt.reshape(n_p, D_MODEL), x_sample.reshape(n_s, D_MODEL)], axis=0)
    xb = x.astype(BF16)
    w_pad = jnp.pad(w_router, ((0, 0), (0, LANES - N_EXPERTS)))
    br_pad = b_router.reshape(N_EXPERTS, 1)
    reps = A_CHUNK // seq_s

    k_p, v_p, k_s, v_s, vg_s, c_p, c_s = [], [], [], [], [], [], []
    for layer in range(DEPTH):
        i = layer // 2
        if layer % 2 == 0:
            h3, kv = _even_in_proj(xb, w_in_even, i)
            w_stack = jnp.stack([w_spatial[i], jnp.tile(w_spatial[i][:, :seq_s, :seq_s], (1, reps, reps))])
            bt_stack = jnp.stack([b_spatial[i].T, jnp.tile(b_spatial[i][:, :seq_s], (1, reps)).T])
            mix, vn_s = _mixer_a(h3, w_stack, bt_stack, ln_v_g[i], ln_v_b[i], n_prompt_blk, seq_s)
            mix = _attn_prompt(h3, kv, mix, sinks[i], n_prompt_blk)
            mix = _attn_sample(h3, kv, mix, cache_k[i].reshape(n_streams * WINDOW, KV_WIDTH),
                               cache_v[i].reshape(n_streams * WINDOW, KV_WIDTH), sinks[i],
                               n_prompt_blk, n_streams, seq_s)
            x, xb = _out_proj_ln(mix, w_out_even, i, x, ln_mix_g[layer], ln_mix_b[layer])
            k_p.append(kv[n_p - win_p:n_p, :KV_WIDTH].reshape(n_batch, win_p, N_KV_HEADS, HEAD_DIM))
            v_p.append(kv[n_p - win_p:n_p, KV_WIDTH:].reshape(n_batch, win_p, N_KV_HEADS, HEAD_DIM))
            k_s.append(kv[n_p:, :KV_WIDTH].reshape(n_streams, seq_s, N_KV_HEADS, HEAD_DIM))
            v_s.append(kv[n_p:, KV_WIDTH:].reshape(n_streams, seq_s, N_KV_HEADS, HEAD_DIM))
            vg_s.append(vn_s.reshape(n_streams, seq_s, A_WIDTH))
        else:
            gate_out, z = _odd_in_proj(xb, w_in_odd, i)
            g = _gated_conv(z, gate_out, state_conv[i].reshape(n_streams * (CONV_WIDTH - 1), D_MODEL),
                            conv_w[i], n_prompt_blk, n_streams, seq_s)
            x, xb = _out_proj_ln(g, w_out_odd, i, x, ln_mix_g[layer], ln_mix_b[layer])
            c_p.append(z[n_p - (CONV_WIDTH - 1):n_p].reshape(n_batch, CONV_WIDTH - 1, D_MODEL))
            c_s.append(z[n_p:].reshape(n_streams, seq_s, D_MODEL)[:, seq_s - (CONV_WIDTH - 1):])
        x, xb = _moe_layer(x, w_pad, br_pad, w_gate, w_up, w_down, layer,
                           ln_ffn_g[layer], ln_ffn_b[layer],
                           n_prompt_blk if layer == DEPTH - 1 else None)
    return (x.reshape(n_batch, seq_p, D_MODEL), xb.reshape(n_streams, seq_s, D_MODEL),
            jnp.stack(k_p), jnp.stack(v_p), jnp.stack(k_s), jnp.stack(v_s),
            jnp.stack(vg_s), jnp.stack(c_p), jnp.stack(c_s))
```

```python
import functools

import jax
import jax.numpy as jnp
from jax import lax
from jax.experimental import pallas as pl
from jax.experimental.pallas import tpu as pltpu

F32 = jnp.float32
BF16 = jnp.bfloat16
I32 = jnp.int32

D_MODEL = 2048
DEPTH = 4
CHUNK = 64
A_HEADS = 8
A_HEAD_DIM = 128
A_WIDTH = A_HEADS * A_HEAD_DIM
A_CHUNK = 128
N_HEADS = 16
N_KV_HEADS = 2
HEAD_DIM = 64
Q_WIDTH = N_HEADS * HEAD_DIM
KV_WIDTH = N_KV_HEADS * HEAD_DIM
WINDOW = 128
CONV_WIDTH = 3
N_EXPERTS = 16
N_GROUPS = 4
EXPERTS_PER_GROUP = N_EXPERTS // N_GROUPS
D_EXPERT = 1024
ALPHA = (2 * DEPTH) ** 0.25
LN_EPS = 1e-5
NEG_INF = -1e30
ALIBI_SLOPES = tuple(2.0 ** (-8.0 * h / N_HEADS) for h in range(1, N_HEADS + 1))

LANES = 128
SUBLANES = 8
ROW_BLK = 256
MM_ROWS = 768
OUT_ROWS = 384
MOE_ROWS = 256
W_CHUNKS = 4
W_STAGES = 3
VMEM_LIMIT = 56 * 1024 * 1024


def _params(n_axes, **kw):
    return pltpu.CompilerParams(dimension_semantics=("arbitrary",) * n_axes,
                                vmem_limit_bytes=VMEM_LIMIT, **kw)


def _layer_norm(y, g, b):
    mu = jnp.mean(y, axis=-1, keepdims=True)
    yc = y - mu
    var = jnp.mean(yc * yc, axis=-1, keepdims=True)
    return yc * lax.rsqrt(var + LN_EPS) * g + b


def _odd_in_kernel(x_ref, wo_ref, wi_ref, wh_ref, go_ref, z_ref, wb_ref):
    @pl.when(pl.program_id(1) == 0)
    def _():
        wb_ref[0] = wo_ref[0].astype(BF16)
        wb_ref[1] = wi_ref[0].astype(BF16)
        wb_ref[2] = wh_ref[0].astype(BF16)

    x = x_ref[...]
    go_ref[...] = jnp.dot(x, wb_ref[0], preferred_element_type=F32)
    gate_in = jnp.dot(x, wb_ref[1], preferred_element_type=F32)
    h = jnp.dot(x, wb_ref[2], preferred_element_type=F32)
    z_ref[...] = gate_in * h


def _odd_in_proj(xb, w_in, li):
    t, k = xb.shape
    tn = 512
    nt = D_MODEL // tn
    out = jax.ShapeDtypeStruct((t, D_MODEL), F32)
    return pl.pallas_call(
        _odd_in_kernel,
        grid=(nt, t // MM_ROWS),
        in_specs=[pl.BlockSpec((MM_ROWS, k), lambda j, i: (i, 0)),
                  pl.BlockSpec((1, k, tn), lambda j, i: (li, 0, j)),
                  pl.BlockSpec((1, k, tn), lambda j, i: (li, 0, nt + j)),
                  pl.BlockSpec((1, k, tn), lambda j, i: (li, 0, 2 * nt + j))],
        out_specs=[pl.BlockSpec((MM_ROWS, tn), lambda j, i: (i, j)),
                   pl.BlockSpec((MM_ROWS, tn), lambda j, i: (i, j))],
        out_shape=[out, out],
        scratch_shapes=[pltpu.VMEM((3, k, tn), BF16)],
        compiler_params=_params(2),
        name="odd_in_proj",
    )(xb, w_in, w_in, w_in)


def _load_weight_bf16(w_ref, li, wb_ref, stage_ref, sem):
    kdim = wb_ref.shape[0]
    rows = stage_ref.shape[1]
    n_chunks = kdim // rows

    @pl.when(pl.program_id(0) == 0)
    def _():
        def chunk_copy(c):
            return pltpu.make_async_copy(w_ref.at[li, pl.ds(c * rows, rows)], stage_ref.at[c % 2], sem.at[c % 2])

        chunk_copy(0).start()
        for c in range(n_chunks):
            if c + 1 < n_chunks:
                chunk_copy(c + 1).start()
            chunk_copy(c).wait()
            wb_ref[c * rows:(c + 1) * rows, :] = stage_ref[c % 2].astype(BF16)


def _even_in_kernel(li, x_ref, w_ref, h_ref, kv_ref, wb_ref, stage_ref, sem):
    _load_weight_bf16(w_ref, li, wb_ref, stage_ref, sem)
    out = jnp.dot(x_ref[...].astype(BF16), wb_ref[...], preferred_element_type=F32)
    n_h = h_ref.shape[1]
    h_ref[...] = out[:, :n_h]
    kv_ref[...] = out[:, n_h:]


def _even_in_proj(xb, w_in, li):
    t, k = xb.shape
    n_all = w_in.shape[2]
    n_kv = 2 * KV_WIDTH
    row = lambda i: (i, 0)
    return pl.pallas_call(
        functools.partial(_even_in_kernel, li),
        grid=(t // OUT_ROWS,),
        in_specs=[pl.BlockSpec((OUT_ROWS, k), row), pl.BlockSpec(memory_space=pl.ANY)],
        out_specs=[pl.BlockSpec((OUT_ROWS, n_all - n_kv), row), pl.BlockSpec((OUT_ROWS, n_kv), row)],
        out_shape=[jax.ShapeDtypeStruct((t, n_all - n_kv), F32), jax.ShapeDtypeStruct((t, n_kv), F32)],
        scratch_shapes=[pltpu.VMEM((k, n_all), BF16),
                        pltpu.VMEM((2, k // 8, n_all), F32),
                        pltpu.SemaphoreType.DMA((2,))],
        compiler_params=_params(1),
        name="even_in_proj",
    )(xb, w_in)


def _out_proj_kernel(li, l_ref, w_ref, r_ref, g_ref, b_ref, of_ref, ob_ref, wb_ref, stage_ref, sem):
    _load_weight_bf16(w_ref, li, wb_ref, stage_ref, sem)
    y = ALPHA * r_ref[...] + jnp.dot(l_ref[...], wb_ref[...], preferred_element_type=F32)
    y = _layer_norm(y, g_ref[...], b_ref[...])
    of_ref[...] = y
    ob_ref[...] = y.astype(BF16)


def _out_proj_ln(lhs, w, li, resid, g, b):
    t, kdim = lhs.shape
    row = lambda i: (i, 0)
    vec = lambda i: (0, 0)
    return pl.pallas_call(
        functools.partial(_out_proj_kernel, li),
        grid=(t // OUT_ROWS,),
        in_specs=[pl.BlockSpec((OUT_ROWS, kdim), row),
                  pl.BlockSpec(memory_space=pl.ANY),
                  pl.BlockSpec((OUT_ROWS, D_MODEL), row),
                  pl.BlockSpec((1, D_MODEL), vec),
                  pl.BlockSpec((1, D_MODEL), vec)],
        out_specs=[pl.BlockSpec((OUT_ROWS, D_MODEL), row),
                   pl.BlockSpec((OUT_ROWS, D_MODEL), row)],
        out_shape=[jax.ShapeDtypeStruct((t, D_MODEL), F32),
                   jax.ShapeDtypeStruct((t, D_MODEL), BF16)],
        scratch_shapes=[pltpu.VMEM((kdim, D_MODEL), BF16),
                        pltpu.VMEM((2, kdim // 4, D_MODEL), F32),
                        pltpu.SemaphoreType.DMA((2,))],
        compiler_params=_params(1),
        name="out_proj_ln",
    )(lhs, w, resid, g.reshape(1, -1), b.reshape(1, -1))


def _mixer_a_kernel(n_prompt_blk, seq_s, u_ref, v_ref, w_ref, bt_ref, g_ref, b_ref, a_ref, vn_ref):
    i = pl.program_id(0)
    is_sample = i == n_prompt_blk
    vn = _layer_norm(v_ref[...], g_ref[...], b_ref[...])

    @pl.when(is_sample)
    def _():
        vn_ref[...] = vn

    shift = jnp.where(is_sample, seq_s.bit_length() - 1, A_CHUNK.bit_length() - 1)
    low = jnp.where(is_sample, seq_s - 1, A_CHUNK - 1)
    r = lax.broadcasted_iota(I32, (A_CHUNK, A_CHUNK), 0)
    c = lax.broadcasted_iota(I32, (A_CHUNK, A_CHUNK), 1)
    mask = ((r >> shift) == (c >> shift)) & ((c & low) <= (r & low))
    vh, vl = _split_bf16(vn)
    bt = bt_ref[0]
    for h in range(A_HEADS):
        wh, wl = _split_bf16(jnp.where(mask, w_ref[0, h], 0.0))
        bcol = bt[:, h:h + 1]
        cs = slice(h * A_HEAD_DIM, (h + 1) * A_HEAD_DIM)
        for n in range(ROW_BLK // A_CHUNK):
            rs = slice(n * A_CHUNK, (n + 1) * A_CHUNK)
            s = (jnp.dot(wh, vh[rs, cs], preferred_element_type=F32)
                 + jnp.dot(wh, vl[rs, cs], preferred_element_type=F32)
                 + jnp.dot(wl, vh[rs, cs], preferred_element_type=F32)) + bcol
            a_ref[rs, cs] = (u_ref[rs, cs] * s).astype(BF16)


def _mixer_a(h3, w_stack, bt_stack, ln_g, ln_b, n_prompt_blk, seq_s):
    t = h3.shape[0]
    return pl.pallas_call(
        functools.partial(_mixer_a_kernel, n_prompt_blk, seq_s),
        grid=(t // ROW_BLK,),
        in_specs=[pl.BlockSpec((ROW_BLK, A_WIDTH), lambda i: (i, 0)),
                  pl.BlockSpec((ROW_BLK, A_WIDTH), lambda i: (i, 1)),
                  pl.BlockSpec((1, A_HEADS, A_CHUNK, A_CHUNK), lambda i: (i // n_prompt_blk, 0, 0, 0)),
                  pl.BlockSpec((1, A_CHUNK, A_HEADS), lambda i: (i // n_prompt_blk, 0, 0)),
                  pl.BlockSpec((1, A_WIDTH), lambda i: (0, 0)),
                  pl.BlockSpec((1, A_WIDTH), lambda i: (0, 0))],
        out_specs=[pl.BlockSpec((ROW_BLK, A_WIDTH), lambda i: (i, 0)),
                   pl.BlockSpec((ROW_BLK, A_WIDTH), lambda i: (0, 0))],
        out_shape=[jax.ShapeDtypeStruct((t, 2 * A_WIDTH), BF16),
                   jax.ShapeDtypeStruct((ROW_BLK, A_WIDTH), F32)],
        compiler_params=_params(1),
        name="mixer_a",
    )(h3, h3, w_stack, bt_stack, ln_g.reshape(1, -1), ln_b.reshape(1, -1))


def _attention_heads(q, k, v, neg_dist, mask_bias, sinks_ref, o_ref, row0):
    grp = N_HEADS // N_KV_HEADS
    m_rows = q.shape[0]
    for kv in range(N_KV_HEADS):
        kg = k[:, kv * HEAD_DIM:(kv + 1) * HEAD_DIM]
        vg = v[:, kv * HEAD_DIM:(kv + 1) * HEAD_DIM]
        outs = []
        for j in range(grp):
            h = kv * grp + j
            qh = q[:, h * HEAD_DIM:(h + 1) * HEAD_DIM]
            s = lax.dot_general(qh, kg, (((1,), (1,)), ((), ())), preferred_element_type=F32)
            s = s + (ALIBI_SLOPES[h] * neg_dist + mask_bias)
            sink = sinks_ref[h]
            m = jnp.maximum(jnp.max(s, axis=-1, keepdims=True), sink)
            p = jnp.exp(s - m)
            den = jnp.sum(p, axis=-1, keepdims=True) + jnp.exp(sink - m)
            outs.append(jnp.dot(p.astype(BF16), vg, preferred_element_type=F32) / den)
        for j in range(0, grp, 2):
            c0 = (kv * grp + j) * HEAD_DIM
            o_ref[row0:row0 + m_rows, c0:c0 + 2 * HEAD_DIM] = jnp.concatenate(outs[j:j + 2], axis=1).astype(BF16)


def _attn_prompt_kernel(sinks_ref, q_ref, kv_ref, halo_ref, mix_ref, o_ref):
    del mix_ref
    i = pl.program_id(0)
    kv_all = jnp.concatenate([halo_ref[...], kv_ref[...]], axis=0).astype(BF16)
    half = WINDOW
    n_keys = 2 * WINDOW
    r = lax.broadcasted_iota(I32, (half, n_keys), 0)
    c = lax.broadcasted_iota(I32, (half, n_keys), 1)
    neg_dist = -jnp.abs(r + WINDOW - c).astype(F32)
    qc = r // CHUNK
    kc = c // CHUNK
    band = (kc >= qc) & (kc <= qc + WINDOW // CHUNK)
    for hf in range(ROW_BLK // half):
        visible = band & (c >= jnp.where(i > 0, 0, WINDOW)) if hf == 0 else band
        keys = kv_all[hf * half:hf * half + n_keys]
        q = (q_ref[hf * half:(hf + 1) * half, :] * (HEAD_DIM ** -0.5)).astype(BF16)
        _attention_heads(q, keys[:, :KV_WIDTH], keys[:, KV_WIDTH:], neg_dist,
                         jnp.where(visible, 0.0, NEG_INF), sinks_ref, o_ref, hf * half)


def _attn_prompt(h3, kv, mix, sinks, n_prompt_blk):
    t = h3.shape[0]
    halo_per_blk = ROW_BLK // WINDOW
    return pl.pallas_call(
        _attn_prompt_kernel,
        grid=(n_prompt_blk,),
        in_specs=[pl.BlockSpec(memory_space=pltpu.SMEM),
                  pl.BlockSpec((ROW_BLK, Q_WIDTH), lambda i: (i, 2)),
                  pl.BlockSpec((ROW_BLK, 2 * KV_WIDTH), lambda i: (i, 0)),
                  pl.BlockSpec((WINDOW, 2 * KV_WIDTH), lambda i: (jnp.maximum(i * halo_per_blk - 1, 0), 0)),
                  pl.BlockSpec(memory_space=pl.ANY)],
        out_specs=pl.BlockSpec((ROW_BLK, Q_WIDTH), lambda i: (i, 1)),
        out_shape=jax.ShapeDtypeStruct((t, 2 * Q_WIDTH), BF16),
        input_output_aliases={4: 0},
        compiler_params=_params(1),
        name="attn_prompt",
    )(sinks, h3, kv, kv, mix)


def _attn_sample_kernel(n_streams, seq_s, sinks_ref, q_ref, kv_ref, ck_ref, cv_ref, mix_ref, o_ref):
    del mix_ref
    n_cache = ck_ref.shape[0]
    w_c = n_cache // n_streams
    k_all = jnp.concatenate([ck_ref[...], kv_ref[:, :KV_WIDTH]], axis=0).astype(BF16)
    v_all = jnp.concatenate([cv_ref[...], kv_ref[:, KV_WIDTH:]], axis=0).astype(BF16)
    n_keys = n_cache + ROW_BLK
    r = lax.broadcasted_iota(I32, (ROW_BLK, n_keys), 0)
    c = lax.broadcasted_iota(I32, (ROW_BLK, n_keys), 1)
    is_new = c >= n_cache
    k_stream = jnp.where(is_new, (c - n_cache) // seq_s, c // w_c)
    k_pos = jnp.where(is_new, w_c + (c - n_cache) % seq_s, c % w_c)
    neg_dist = -jnp.abs(w_c + r % seq_s - k_pos).astype(F32)
    mask_bias = jnp.where(k_stream == r // seq_s, 0.0, NEG_INF)
    q = (q_ref[...] * (HEAD_DIM ** -0.5)).astype(BF16)
    _attention_heads(q, k_all, v_all, neg_dist, mask_bias, sinks_ref, o_ref, 0)


def _attn_sample(h3, kv, mix, cache_k2d, cache_v2d, sinks, n_prompt_blk, n_streams, seq_s):
    t = h3.shape[0]
    full = lambda i: (0, 0)
    return pl.pallas_call(
        functools.partial(_attn_sample_kernel, n_streams, seq_s),
        grid=(1,),
        in_specs=[pl.BlockSpec(memory_space=pltpu.SMEM),
                  pl.BlockSpec((ROW_BLK, Q_WIDTH), lambda i: (n_prompt_blk, 2)),
                  pl.BlockSpec((ROW_BLK, 2 * KV_WIDTH), lambda i: (n_prompt_blk, 0)),
                  pl.BlockSpec(cache_k2d.shape, full),
                  pl.BlockSpec(cache_v2d.shape, full),
                  pl.BlockSpec(memory_space=pl.ANY)],
        out_specs=pl.BlockSpec((ROW_BLK, Q_WIDTH), lambda i: (n_prompt_blk, 1)),
        out_shape=jax.ShapeDtypeStruct((t, 2 * Q_WIDTH), BF16),
        input_output_aliases={5: 0},
        compiler_params=_params(1),
        name="attn_sample",
    )(sinks, h3, kv, cache_k2d, cache_v2d, mix)


def _conv_kernel(n_prompt_blk, n_streams, seq_s, z_ref, halo_ref, go_ref, st_ref, cw_ref, o_ref):
    i = pl.program_id(0)
    z = z_ref[...]
    rows = lax.broadcasted_iota(I32, (ROW_BLK, 1), 0)
    zm1 = pltpu.roll(z, 1, 0)
    zm2 = pltpu.roll(z, 2, 0)

    def finish(zm1, zm2):
        conv = cw_ref[0:1, :] * zm2 + cw_ref[1:2, :] * zm1 + cw_ref[2:3, :] * z
        o_ref[...] = (go_ref[...] * conv).astype(BF16)

    @pl.when(i < n_prompt_blk)
    def _():
        live = jnp.where(i > 0, 1.0, 0.0)
        p1 = halo_ref[7:8, :] * live
        p2 = halo_ref[6:7, :] * live
        finish(jnp.where(rows == 0, p1, zm1),
               jnp.where(rows == 0, p2, jnp.where(rows == 1, p1, zm2)))

    @pl.when(i == n_prompt_blk)
    def _():
        a, b = zm1, zm2
        for s in range(n_streams):
            s0 = st_ref[2 * s:2 * s + 1, :]
            s1 = st_ref[2 * s + 1:2 * s + 2, :]
            a = jnp.where(rows == s * seq_s, s1, a)
            b = jnp.where(rows == s * seq_s, s0, jnp.where(rows == s * seq_s + 1, s1, b))
        finish(a, b)


def _gated_conv(z, gate_out, state2d, conv_w, n_prompt_blk, n_streams, seq_s):
    t = z.shape[0]
    halo_per_blk = ROW_BLK // 8
    return pl.pallas_call(
        functools.partial(_conv_kernel, n_prompt_blk, n_streams, seq_s),
        grid=(t // ROW_BLK,),
        in_specs=[pl.BlockSpec((ROW_BLK, D_MODEL), lambda i: (i, 0)),
                  pl.BlockSpec((8, D_MODEL), lambda i: (jnp.maximum(i * halo_per_blk - 1, 0), 0)),
                  pl.BlockSpec((ROW_BLK, D_MODEL), lambda i: (i, 0)),
                  pl.BlockSpec(state2d.shape, lambda i: (0, 0)),
                  pl.BlockSpec(conv_w.shape, lambda i: (0, 0))],
        out_specs=pl.BlockSpec((ROW_BLK, D_MODEL), lambda i: (i, 0)),
        out_shape=jax.ShapeDtypeStruct((t, D_MODEL), BF16),
        compiler_params=_params(1),
        name="gated_conv",
    )(z, z, gate_out, state2d, conv_w)


def _split_bf16(x):
    hi = x.astype(BF16)
    lo = (x - hi.astype(F32)).astype(BF16)
    return hi, lo


def _router_kernel(n_blk, x_ref, w_ref, br_ref, ri_ref, gate_ref, cnt_ref, carry_ref, wsplit_ref):
    i = pl.program_id(0)

    @pl.when(i == 0)
    def _():
        carry_ref[...] = jnp.zeros_like(carry_ref)
        wh, wl = _split_bf16(w_ref[...])
        wsplit_ref[0] = wh
        wsplit_ref[1] = wl

    xh, xl = _split_bf16(x_ref[...])
    wh, wl = wsplit_ref[0], wsplit_ref[1]
    logits = (jnp.dot(xh, wh, preferred_element_type=F32) + jnp.dot(xh, wl, preferred_element_type=F32)
              + jnp.dot(xl, wh, preferred_element_type=F32))
    lt = logits.T[:N_EXPERTS]
    ex = jnp.exp(lt - jnp.max(lt, axis=0, keepdims=True))
    scores = ex / jnp.sum(ex, axis=0, keepdims=True)
    biased = scores + br_ref[...]
    sc = [scores[e:e + 1] for e in range(N_EXPERTS)]
    bs = [biased[e:e + 1] for e in range(N_EXPERTS)]

    gscore = []
    for g in range(N_GROUPS):
        a, b, c, d = bs[4 * g:4 * g + 4]
        hi1, lo1 = jnp.maximum(a, b), jnp.minimum(a, b)
        hi2, lo2 = jnp.maximum(c, d), jnp.minimum(c, d)
        gscore.append(jnp.maximum(hi1, hi2) + jnp.maximum(jnp.minimum(hi1, hi2), jnp.maximum(lo1, lo2)))
    best, grp = gscore[0], jnp.zeros((1, ROW_BLK), I32)
    for g in range(1, N_GROUPS):
        better = gscore[g] > best
        grp = jnp.where(better, g, grp)
        best = jnp.where(better, gscore[g], best)

    def pick(rows, idx, n):
        out = rows[0]
        for j in range(1, n):
            out = jnp.where(idx == j, rows[j], out)
        return out

    in_b = [pick([bs[4 * g + j] for g in range(N_GROUPS)], grp, N_GROUPS) for j in range(EXPERTS_PER_GROUP)]
    in_s = [pick([sc[4 * g + j] for g in range(N_GROUPS)], grp, N_GROUPS) for j in range(EXPERTS_PER_GROUP)]
    v1, l1 = in_b[0], jnp.zeros((1, ROW_BLK), I32)
    for j in range(1, EXPERTS_PER_GROUP):
        better = in_b[j] > v1
        l1 = jnp.where(better, j, l1)
        v1 = jnp.where(better, in_b[j], v1)
    v2, l2 = jnp.full((1, ROW_BLK), -jnp.inf, F32), jnp.zeros((1, ROW_BLK), I32)
    for j in range(EXPERTS_PER_GROUP):
        better = (l1 != j) & (in_b[j] > v2)
        l2 = jnp.where(better, j, l2)
        v2 = jnp.where(better, in_b[j], v2)
    gate1 = pick(in_s, l1, EXPERTS_PER_GROUP)
    gate2 = pick(in_s, l2, EXPERTS_PER_GROUP)
    gsum = gate1 + gate2
    gate_ref[0:1, :] = gate1 / gsum
    gate_ref[1:2, :] = gate2 / gsum
    e1 = grp * EXPERTS_PER_GROUP + l1
    e2 = grp * EXPERTS_PER_GROUP + l2
    ri_ref[0:1, :] = e1
    ri_ref[1:2, :] = e2

    expert = lax.broadcasted_iota(I32, (N_EXPERTS, ROW_BLK), 0)
    oh1 = (expert == e1).astype(F32)
    oh2 = (expert == e2).astype(F32)
    r = lax.broadcasted_iota(I32, (ROW_BLK, ROW_BLK), 0)
    c = lax.broadcasted_iota(I32, (ROW_BLK, ROW_BLK), 1)
    before = (r < c).astype(BF16)
    cs = jnp.dot(jnp.concatenate([oh1, oh2], axis=0).astype(BF16), before, preferred_element_type=F32)
    tot1 = jnp.sum(oh1, axis=1, keepdims=True)
    tot2 = jnp.sum(oh2, axis=1, keepdims=True)
    carry = carry_ref[...][:, 0:1]
    ri_ref[2:3, :] = jnp.sum(oh1 * (carry + cs[:N_EXPERTS]), axis=0, keepdims=True).astype(I32)
    ri_ref[3:4, :] = jnp.sum(oh2 * (carry + tot1 + cs[N_EXPERTS:]), axis=0, keepdims=True).astype(I32)
    carry_ref[...] = carry_ref[...] + tot1 + tot2

    @pl.when(i == n_blk - 1)
    def _():
        cnt_ref[...] = carry_ref[...]


def _router(x, w_pad, b_col):
    t = x.shape[0]
    n_blk = t // ROW_BLK
    return pl.pallas_call(
        functools.partial(_router_kernel, n_blk),
        grid=(n_blk,),
        in_specs=[pl.BlockSpec((ROW_BLK, D_MODEL), lambda i: (i, 0)),
                  pl.BlockSpec((D_MODEL, LANES), lambda i: (0, 0)),
                  pl.BlockSpec((N_EXPERTS, 1), lambda i: (0, 0))],
        out_specs=[pl.BlockSpec((4, ROW_BLK), lambda i: (0, i)),
                   pl.BlockSpec((2, ROW_BLK), lambda i: (0, i)),
                   pl.BlockSpec((N_EXPERTS, LANES), lambda i: (0, 0))],
        out_shape=[jax.ShapeDtypeStruct((4, t), I32),
                   jax.ShapeDtypeStruct((2, t), F32),
                   jax.ShapeDtypeStruct((N_EXPERTS, LANES), F32)],
        scratch_shapes=[pltpu.VMEM((N_EXPERTS, LANES), F32), pltpu.VMEM((2, D_MODEL, LANES), BF16)],
        compiler_params=_params(1),
        name="router",
    )(x, w_pad, b_col)


def _dispatch_kernel(pos_ref, meta_ref, x_ref, xs_ref, sem):
    i = pl.program_id(0)
    base = i * ROW_BLK

    def row_copy(src_row, dst_row):
        return pltpu.make_async_copy(x_ref.at[pl.ds(src_row, 1)], xs_ref.at[pl.ds(dst_row, 1)], sem)

    def group(g, carry):
        tile_rows = x_ref.at[pl.ds(pl.multiple_of(g * SUBLANES, SUBLANES), SUBLANES)]
        for j in range(SUBLANES):
            t = base + g * SUBLANES + j
            for k in range(2):
                pltpu.make_async_copy(tile_rows.at[pl.ds(j, 1)], xs_ref.at[pl.ds(pos_ref[2 * t + k], 1)],
                                      sem).start(priority=k)
        return carry

    lax.fori_loop(0, ROW_BLK // SUBLANES, group, 0)

    @pl.when(i == 0)
    def _():
        for e in range(N_EXPERTS):
            cnt, start, padded = meta_ref[e], meta_ref[N_EXPERTS + e], meta_ref[2 * N_EXPERTS + e]

            def fill(ri, carry):
                row_copy(0, start + ri).start()
                return carry

            def drain(ri, carry):
                row_copy(0, 0).wait()
                return carry

            lax.fori_loop(cnt, padded, fill, 0)
            lax.fori_loop(cnt, padded, drain, 0)

    for _ in range(2):
        pltpu.make_async_copy(x_ref, xs_ref.at[pl.ds(0, ROW_BLK)], sem).wait()


def _dispatch(pos_flat, meta, x, n_rows):
    t = x.shape[0]
    grid_spec = pltpu.PrefetchScalarGridSpec(
        num_scalar_prefetch=2,
        grid=(t // ROW_BLK,),
        in_specs=[pl.BlockSpec((ROW_BLK, D_MODEL), lambda i, pos, meta: (i, 0))],
        out_specs=pl.BlockSpec(memory_space=pl.ANY),
        scratch_shapes=[pltpu.SemaphoreType.DMA(())],
    )
    return pl.pallas_call(
        _dispatch_kernel,
        grid_spec=grid_spec,
        out_shape=jax.ShapeDtypeStruct((n_rows, D_MODEL), x.dtype),
        compiler_params=_params(1),
        name="dispatch",
    )(pos_flat, meta, x)


def _expert_mlp_kernel(li, first_ref, nt_ref, row0_ref, slot_ref, next_ref, xs_ref, wg_ref, wu_ref, wd_ref,
                       ys_ref, wgb, wub, wdb, sg, su, sd, xstage, ostage, w_sem, x_sem, y_sem):
    e = pl.program_id(0)
    nt = nt_ref[e]
    row0 = row0_ref[e]
    slot = slot_ref[e]
    e_next = next_ref[e]
    prefetch = e_next >= 0
    w_parts = ((wg_ref, sg, wgb, D_MODEL // W_CHUNKS),
               (wu_ref, su, wub, D_MODEL // W_CHUNKS),
               (wd_ref, sd, wdb, D_EXPERT // W_CHUNKS))

    def w_copy(k, expert, c):
        src, stage, _, rows = w_parts[k]
        s = c % W_STAGES
        return pltpu.make_async_copy(src.at[li, expert, pl.ds(pl.multiple_of(c * rows, rows), rows)],
                                     stage.at[s], w_sem.at[k, s])

    def w_start(expert):
        for c in range(W_STAGES - 1):
            for k in range(3):
                w_copy(k, expert, c).start()

    def w_chunk(expert, c, dst_slot):
        for k in range(3):
            w_copy(k, expert, c).wait()

        @pl.when(c + W_STAGES - 1 < W_CHUNKS)
        def _():
            for k in range(3):
                w_copy(k, expert, c + W_STAGES - 1).start()

        for _, stage, dst, rows in w_parts:
            dst[dst_slot, pl.ds(pl.multiple_of(c * rows, rows), rows), :] = stage[c % W_STAGES].astype(BF16)

    def hbm_rows(r):
        return pl.ds(pl.multiple_of(row0 + r * MOE_ROWS, MOE_ROWS), MOE_ROWS)

    def x_load(r, s):
        return pltpu.make_async_copy(xs_ref.at[hbm_rows(r)], xstage.at[s], x_sem.at[s])

    def y_store(r, s):
        return pltpu.make_async_copy(ostage.at[s], ys_ref.at[hbm_rows(r)], y_sem.at[s])

    def tile(r):
        s = r % 2
        x_load(r, s).wait()

        @pl.when(r + 1 < nt)
        def _():
            x_load(r + 1, 1 - s).start()

        xb = xstage[s].astype(BF16)
        g = jnp.dot(xb, wgb[slot], preferred_element_type=F32)
        u = jnp.dot(xb, wub[slot], preferred_element_type=F32)
        h = (g * jax.nn.sigmoid(g) * u).astype(BF16)
        out = jnp.dot(h, wdb[slot], preferred_element_type=F32)

        @pl.when(r >= 2)
        def _():
            y_store(r - 2, s).wait()

        ostage[s] = out
        y_store(r, s).start()

    @pl.when(e == first_ref[0])
    def _():
        w_start(e)

        def own(c, carry):
            w_chunk(e, c, slot)
            return carry

        lax.fori_loop(0, W_CHUNKS, own, 0)

    @pl.when(nt > 0)
    def _():
        @pl.when(prefetch)
        def _():
            w_start(e_next)

        x_load(0, 0).start()

        def step(r, carry):
            @pl.when(r < nt)
            def _():
                tile(r)

            @pl.when(prefetch & (r < W_CHUNKS))
            def _():
                w_chunk(e_next, r, 1 - slot)

            return carry

        lax.fori_loop(0, jnp.maximum(nt, jnp.where(prefetch, W_CHUNKS, 0)), step, 0)
        y_store(0, (nt - 1) % 2).wait()

        @pl.when(nt >= 2)
        def _():
            y_store(0, nt % 2).wait()


def _expert_mlp(first, tiles_e, starts, slot_e, next_e, xs, w_gate, w_up, w_down, li):
    n_rows = xs.shape[0]
    qg, qd = D_MODEL // W_CHUNKS, D_EXPERT // W_CHUNKS
    grid_spec = pltpu.PrefetchScalarGridSpec(
        num_scalar_prefetch=5,
        grid=(N_EXPERTS,),
        in_specs=[pl.BlockSpec(memory_space=pl.ANY)] * 4,
        out_specs=pl.BlockSpec(memory_space=pl.ANY),
        scratch_shapes=[pltpu.VMEM((2, D_MODEL, D_EXPERT), BF16),
                        pltpu.VMEM((2, D_MODEL, D_EXPERT), BF16),
                        pltpu.VMEM((2, D_EXPERT, D_MODEL), BF16),
                        pltpu.VMEM((W_STAGES, qg, D_EXPERT), F32),
                        pltpu.VMEM((W_STAGES, qg, D_EXPERT), F32),
                        pltpu.VMEM((W_STAGES, qd, D_MODEL), F32),
                        pltpu.VMEM((2, MOE_ROWS, D_MODEL), xs.dtype),
                        pltpu.VMEM((2, MOE_ROWS, D_MODEL), F32),
                        pltpu.SemaphoreType.DMA((3, W_STAGES)),
                        pltpu.SemaphoreType.DMA((2,)),
                        pltpu.SemaphoreType.DMA((2,))],
    )
    return pl.pallas_call(
        functools.partial(_expert_mlp_kernel, li),
        grid_spec=grid_spec,
        out_shape=jax.ShapeDtypeStruct((n_rows, D_MODEL), F32),
        compiler_params=_params(1),
        name="expert_mlp",
    )(first, tiles_e, starts, slot_e, next_e, xs, w_gate, w_up, w_down)


def _combine_kernel(n_prompt_blk, pos_ref, ys_ref, x_ref, gate_ref, g_ref, b_ref, oa_ref, ob_ref, buf_ref, sem):
    i = pl.program_id(0)
    slot = i % 2

    def gather(blk, s):
        def group(g, carry):
            for k in range(2):
                tile_rows = buf_ref.at[s, k, pl.ds(pl.multiple_of(g * SUBLANES, SUBLANES), SUBLANES)]
                for j in range(SUBLANES):
                    t = blk * ROW_BLK + g * SUBLANES + j
                    pltpu.make_async_copy(ys_ref.at[pl.ds(pos_ref[2 * t + k], 1)],
                                          tile_rows.at[pl.ds(j, 1)], sem.at[s]).start(priority=k)
            return carry

        lax.fori_loop(0, ROW_BLK // SUBLANES, group, 0)

    @pl.when(i == 0)
    def _():
        gather(0, 0)

    @pl.when(i + 1 < pl.num_programs(0))
    def _():
        gather(i + 1, 1 - slot)

    for k in range(2):
        pltpu.make_async_copy(ys_ref.at[pl.ds(0, ROW_BLK)], buf_ref.at[slot, k], sem.at[slot]).wait()
    gates = gate_ref[...]
    y = ALPHA * x_ref[...] + gates[:, 0:1] * buf_ref[slot, 0] + gates[:, 1:2] * buf_ref[slot, 1]
    y = _layer_norm(y, g_ref[...], b_ref[...])
    if n_prompt_blk is None:
        oa_ref[...] = y
        ob_ref[...] = y.astype(BF16)
    else:
        @pl.when(i < n_prompt_blk)
        def _():
            oa_ref[...] = y

        @pl.when(i == n_prompt_blk)
        def _():
            ob_ref[...] = y


def _combine_ln(pos_flat, ys, x, gates, g, b, n_prompt_blk=None):
    t = x.shape[0]
    row = lambda i, pos: (i, 0)
    vec = lambda i, pos: (0, 0)
    if n_prompt_blk is None:
        out_specs = [pl.BlockSpec((ROW_BLK, D_MODEL), row), pl.BlockSpec((ROW_BLK, D_MODEL), row)]
        out_shape = [jax.ShapeDtypeStruct((t, D_MODEL), F32), jax.ShapeDtypeStruct((t, D_MODEL), BF16)]
    else:
        out_specs = [pl.BlockSpec((ROW_BLK, D_MODEL), lambda i, pos: (jnp.minimum(i, n_prompt_blk - 1), 0)),
                     pl.BlockSpec((ROW_BLK, D_MODEL), vec)]
        out_shape = [jax.ShapeDtypeStruct((n_prompt_blk * ROW_BLK, D_MODEL), F32),
                     jax.ShapeDtypeStruct((t - n_prompt_blk * ROW_BLK, D_MODEL), F32)]
    grid_spec = pltpu.PrefetchScalarGridSpec(
        num_scalar_prefetch=1,
        grid=(t // ROW_BLK,),
        in_specs=[pl.BlockSpec(memory_space=pl.ANY),
                  pl.BlockSpec((ROW_BLK, D_MODEL), row),
                  pl.BlockSpec((ROW_BLK, 2), row),
                  pl.BlockSpec((1, D_MODEL), vec),
                  pl.BlockSpec((1, D_MODEL), vec)],
        out_specs=out_specs,
        scratch_shapes=[pltpu.VMEM((2, 2, ROW_BLK, D_MODEL), F32), pltpu.SemaphoreType.DMA((2,))],
    )
    return pl.pallas_call(
        functools.partial(_combine_kernel, n_prompt_blk),
        grid_spec=grid_spec,
        out_shape=out_shape,
        compiler_params=_params(1),
        name="combine_ln",
    )(pos_flat, ys, x, gates, g.reshape(1, -1), b.reshape(1, -1))


def _moe_layer(x, w_pad, br_pad, w_gate, w_up, w_down, li, ln_g, ln_b, n_prompt_blk=None):
    t = x.shape[0]
    n_tiles = -(-(2 * t + N_EXPERTS * (MOE_ROWS - 1)) // MOE_ROWS)
    ri, gates, cnt = _router(x, w_pad, br_pad)
    gates = gates.T
    counts = cnt[:, 0].astype(I32)
    tiles_e = (counts + MOE_ROWS - 1) // MOE_ROWS
    padded = tiles_e * MOE_ROWS
    starts = (jnp.cumsum(padded) - padded).astype(I32)
    onehot = ri[0:2, :, None] == jnp.arange(N_EXPERTS, dtype=I32)
    pos = jnp.sum(jnp.where(onehot, starts, 0), axis=-1) + ri[2:4]
    pos_flat = pos.T.reshape(-1)
    meta = jnp.concatenate([counts, starts, padded]).astype(I32)
    ids = jnp.arange(N_EXPERTS, dtype=I32)
    live = tiles_e > 0
    slot_e = ((jnp.cumsum(live) - live) % 2).astype(I32)
    later = live[None, :] & (ids[None, :] > ids[:, None])
    next_e = jnp.min(jnp.where(later, ids[None, :], N_EXPERTS), axis=1)
    next_e = jnp.where(next_e < N_EXPERTS, next_e, -1).astype(I32)
    first = jnp.min(jnp.where(live, ids, N_EXPERTS - 1)).astype(I32).reshape(1)

    xs = _dispatch(pos_flat, meta, x, n_tiles * MOE_ROWS)
    ys = _expert_mlp(first, tiles_e.astype(I32), starts, slot_e, next_e, xs, w_gate, w_up, w_down, li)
    return _combine_ln(pos_flat, ys, x, gates, ln_g, ln_b, n_prompt_blk)


def kernel(x_prompt, x_sample, cache_k, cache_v, state_conv, w_in_even, ln_v_g, ln_v_b, w_spatial, b_spatial, sinks, w_out_even, w_in_odd, conv_w, w_out_odd, ln_mix_g, ln_mix_b, ln_ffn_g, ln_ffn_b, w_router, b_router, w_gate, w_up, w_down):
    n_batch, seq_p, _ = x_prompt.shape
    n_streams, seq_s, _ = x_sample.shape
    n_p = n_batch * seq_p
    n_s = n_streams * seq_s
    assert n_batch == 1 and n_s == ROW_BLK and n_p % ROW_BLK == 0 and (n_p + n_s) % MM_ROWS == 0
    assert seq_s & (seq_s - 1) == 0 and A_CHUNK % seq_s == 0 and cache_k.shape[2] == WINDOW
    n_prompt_blk = n_p // ROW_BLK
    win_p = min(WINDOW, seq_p)

    x = jnp.concatenate([x_prompt.reshape(n_p, D_MODEL), x_sample.reshape(n_s, D_MODEL)], axis=0)
    xb = x
    w_pad = jnp.pad(w_router, ((0, 0), (0, LANES - N_EXPERTS)))
    br_pad = b_router.reshape(N_EXPERTS, 1)
    reps = A_CHUNK // seq_s

    k_p, v_p, k_s, v_s, vg_s, c_p, c_s = [], [], [], [], [], [], []
    for layer in range(DEPTH):
        i = layer // 2
        if layer % 2 == 0:
            h3, kv = _even_in_proj(xb, w_in_even, i)
            w_stack = jnp.stack([w_spatial[i], jnp.tile(w_spatial[i][:, :seq_s, :seq_s], (1, reps, reps))])
            bt_stack = jnp.stack([b_spatial[i].T, jnp.tile(b_spatial[i][:, :seq_s], (1, reps)).T])
            mix, vn_s = _mixer_a(h3, w_stack, bt_stack, ln_v_g[i], ln_v_b[i], n_prompt_blk, seq_s)
            mix = _attn_prompt(h3, kv, mix, sinks[i], n_prompt_blk)
            mix = _attn_sample(h3, kv, mix, cache_k[i].reshape(n_streams * WINDOW, KV_WIDTH),
                               cache_v[i].reshape(n_streams * WINDOW, KV_WIDTH), sinks[i],
                               n_prompt_blk, n_streams, seq_s)
            x, xb = _out_proj_ln(mix, w_out_even, i, x, ln_mix_g[layer], ln_mix_b[layer])
            k_p.append(kv[n_p - win_p:n_p, :KV_WIDTH].reshape(n_batch, win_p, N_KV_HEADS, HEAD_DIM))
            v_p.append(kv[n_p - win_p:n_p, KV_WIDTH:].reshape(n_batch, win_p, N_KV_HEADS, HEAD_DIM))
            k_s.append(kv[n_p:, :KV_WIDTH].reshape(n_streams, seq_s, N_KV_HEADS, HEAD_DIM))
            v_s.append(kv[n_p:, KV_WIDTH:].reshape(n_streams, seq_s, N_KV_HEADS, HEAD_DIM))
            vg_s.append(vn_s.reshape(n_streams, seq_s, A_WIDTH))
        else:
            gate_out, z = _odd_in_proj(xb, w_in_odd, i)
            g = _gated_conv(z, gate_out, state_conv[i].reshape(n_streams * (CONV_WIDTH - 1), D_MODEL),
                            conv_w[i], n_prompt_blk, n_streams, seq_s)
            x, xb = _out_proj_ln(g, w_out_odd, i, x, ln_mix_g[layer], ln_mix_b[layer])
            c_p.append(z[n_p - (CONV_WIDTH - 1):n_p].reshape(n_batch, CONV_WIDTH - 1, D_MODEL))
            c_s.append(z[n_p:].reshape(n_streams, seq_s, D_MODEL)[:, seq_s - (CONV_WIDTH - 1):])
        x, xb = _moe_layer(x, w_pad, br_pad, w_gate, w_up, w_down, layer,
                           ln_ffn_g[layer], ln_ffn_b[layer],
                           n_prompt_blk if layer == DEPTH - 1 else None)
    return (x.reshape(n_batch, seq_p, D_MODEL), xb.reshape(n_streams, seq_s, D_MODEL),
            jnp.stack(k_p), jnp.stack(v_p), jnp.stack(k_s), jnp.stack(v_s),
            jnp.stack(vg_s), jnp.stack(c_p), jnp.stack(c_s))
```

```python
import functools

import jax
import jax.numpy as jnp
from jax import lax
from jax.experimental import pallas as pl
from jax.experimental.pallas import tpu as pltpu

F32 = jnp.float32
BF16 = jnp.bfloat16
I32 = jnp.int32

D_MODEL = 2048
DEPTH = 4
CHUNK = 64
A_HEADS = 8
A_HEAD_DIM = 128
A_WIDTH = A_HEADS * A_HEAD_DIM
A_CHUNK = 128
N_HEADS = 16
N_KV_HEADS = 2
HEAD_DIM = 64
Q_WIDTH = N_HEADS * HEAD_DIM
KV_WIDTH = N_KV_HEADS * HEAD_DIM
WINDOW = 128
CONV_WIDTH = 3
N_EXPERTS = 16
N_GROUPS = 4
EXPERTS_PER_GROUP = N_EXPERTS // N_GROUPS
D_EXPERT = 1024
ALPHA = (2 * DEPTH) ** 0.25
LN_EPS = 1e-5
NEG_INF = -1e30
ALIBI_SLOPES = tuple(2.0 ** (-8.0 * h / N_HEADS) for h in range(1, N_HEADS + 1))

LANES = 128
SUBLANES = 8
ROW_BLK = 256
MM_ROWS = 768
OUT_ROWS = 384
MOE_ROWS = 256
W_CHUNKS = 4
ROW_TILE_DMA_PRIORITY = 1
W_STAGES = 3
VMEM_LIMIT = 56 * 1024 * 1024


def _params(n_axes, **kw):
    return pltpu.CompilerParams(dimension_semantics=("arbitrary",) * n_axes,
                                vmem_limit_bytes=VMEM_LIMIT, **kw)


def _layer_norm(y, g, b):
    mu = jnp.mean(y, axis=-1, keepdims=True)
    yc = y - mu
    var = jnp.mean(yc * yc, axis=-1, keepdims=True)
    return yc * lax.rsqrt(var + LN_EPS) * g + b


def _odd_in_kernel(x_ref, wo_ref, wi_ref, wh_ref, go_ref, z_ref, wb_ref):
    @pl.when(pl.program_id(1) == 0)
    def _():
        wb_ref[0] = wo_ref[0].astype(BF16)
        wb_ref[1] = wi_ref[0].astype(BF16)
        wb_ref[2] = wh_ref[0].astype(BF16)

    x = x_ref[...]
    go_ref[...] = jnp.dot(x, wb_ref[0], preferred_element_type=F32)
    gate_in = jnp.dot(x, wb_ref[1], preferred_element_type=F32)
    h = jnp.dot(x, wb_ref[2], preferred_element_type=F32)
    z_ref[...] = gate_in * h


def _odd_in_proj(xb, w_in, li):
    t, k = xb.shape
    tn = 512
    nt = D_MODEL // tn
    out = jax.ShapeDtypeStruct((t, D_MODEL), F32)
    return pl.pallas_call(
        _odd_in_kernel,
        grid=(nt, t // MM_ROWS),
        in_specs=[pl.BlockSpec((MM_ROWS, k), lambda j, i: (i, 0)),
                  pl.BlockSpec((1, k, tn), lambda j, i: (li, 0, j)),
                  pl.BlockSpec((1, k, tn), lambda j, i: (li, 0, nt + j)),
                  pl.BlockSpec((1, k, tn), lambda j, i: (li, 0, 2 * nt + j))],
        out_specs=[pl.BlockSpec((MM_ROWS, tn), lambda j, i: (i, j)),
                   pl.BlockSpec((MM_ROWS, tn), lambda j, i: (i, j))],
        out_shape=[out, out],
        scratch_shapes=[pltpu.VMEM((3, k, tn), BF16)],
        compiler_params=_params(2),
        name="odd_in_proj",
    )(xb, w_in, w_in, w_in)


def _load_weight_bf16(w_ref, li, wb_ref, stage_ref, sem):
    kdim = wb_ref.shape[0]
    rows = stage_ref.shape[1]
    n_chunks = kdim // rows

    @pl.when(pl.program_id(0) == 0)
    def _():
        def chunk_copy(c):
            return pltpu.make_async_copy(w_ref.at[li, pl.ds(c * rows, rows)], stage_ref.at[c % 2], sem.at[c % 2])

        chunk_copy(0).start()
        for c in range(n_chunks):
            if c + 1 < n_chunks:
                chunk_copy(c + 1).start()
            chunk_copy(c).wait()
            wb_ref[c * rows:(c + 1) * rows, :] = stage_ref[c % 2].astype(BF16)


def _even_in_kernel(li, x_ref, w_ref, h_ref, kv_ref, wb_ref, stage_ref, sem):
    _load_weight_bf16(w_ref, li, wb_ref, stage_ref, sem)
    out = jnp.dot(x_ref[...].astype(BF16), wb_ref[...], preferred_element_type=F32)
    n_h = h_ref.shape[1]
    h_ref[...] = out[:, :n_h]
    kv_ref[...] = out[:, n_h:]


def _even_in_proj(xb, w_in, li):
    t, k = xb.shape
    n_all = w_in.shape[2]
    n_kv = 2 * KV_WIDTH
    row = lambda i: (i, 0)
    return pl.pallas_call(
        functools.partial(_even_in_kernel, li),
        grid=(t // OUT_ROWS,),
        in_specs=[pl.BlockSpec((OUT_ROWS, k), row), pl.BlockSpec(memory_space=pl.ANY)],
        out_specs=[pl.BlockSpec((OUT_ROWS, n_all - n_kv), row), pl.BlockSpec((OUT_ROWS, n_kv), row)],
        out_shape=[jax.ShapeDtypeStruct((t, n_all - n_kv), F32), jax.ShapeDtypeStruct((t, n_kv), F32)],
        scratch_shapes=[pltpu.VMEM((k, n_all), BF16),
                        pltpu.VMEM((2, k // 8, n_all), F32),
                        pltpu.SemaphoreType.DMA((2,))],
        compiler_params=_params(1),
        name="even_in_proj",
    )(xb, w_in)


def _out_proj_kernel(li, l_ref, w_ref, r_ref, g_ref, b_ref, of_ref, ob_ref, wb_ref, stage_ref, sem):
    _load_weight_bf16(w_ref, li, wb_ref, stage_ref, sem)
    y = ALPHA * r_ref[...] + jnp.dot(l_ref[...], wb_ref[...], preferred_element_type=F32)
    y = _layer_norm(y, g_ref[...], b_ref[...])
    of_ref[...] = y
    ob_ref[...] = y.astype(BF16)


def _out_proj_ln(lhs, w, li, resid, g, b):
    t, kdim = lhs.shape
    row = lambda i: (i, 0)
    vec = lambda i: (0, 0)
    return pl.pallas_call(
        functools.partial(_out_proj_kernel, li),
        grid=(t // OUT_ROWS,),
        in_specs=[pl.BlockSpec((OUT_ROWS, kdim), row),
                  pl.BlockSpec(memory_space=pl.ANY),
                  pl.BlockSpec((OUT_ROWS, D_MODEL), row),
                  pl.BlockSpec((1, D_MODEL), vec),
                  pl.BlockSpec((1, D_MODEL), vec)],
        out_specs=[pl.BlockSpec((OUT_ROWS, D_MODEL), row),
                   pl.BlockSpec((OUT_ROWS, D_MODEL), row)],
        out_shape=[jax.ShapeDtypeStruct((t, D_MODEL), F32),
                   jax.ShapeDtypeStruct((t, D_MODEL), BF16)],
        scratch_shapes=[pltpu.VMEM((kdim, D_MODEL), BF16),
                        pltpu.VMEM((2, kdim // 4, D_MODEL), F32),
                        pltpu.SemaphoreType.DMA((2,))],
        compiler_params=_params(1),
        name="out_proj_ln",
    )(lhs, w, resid, g.reshape(1, -1), b.reshape(1, -1))


def _mixer_a_kernel(n_prompt_blk, seq_s, u_ref, v_ref, w_ref, bt_ref, g_ref, b_ref, a_ref, vn_ref):
    i = pl.program_id(0)
    is_sample = i == n_prompt_blk
    vn = _layer_norm(v_ref[...], g_ref[...], b_ref[...])

    @pl.when(is_sample)
    def _():
        vn_ref[...] = vn

    shift = jnp.where(is_sample, seq_s.bit_length() - 1, A_CHUNK.bit_length() - 1)
    low = jnp.where(is_sample, seq_s - 1, A_CHUNK - 1)
    r = lax.broadcasted_iota(I32, (A_CHUNK, A_CHUNK), 0)
    c = lax.broadcasted_iota(I32, (A_CHUNK, A_CHUNK), 1)
    mask = ((r >> shift) == (c >> shift)) & ((c & low) <= (r & low))
    vh, vl = _split_bf16(vn)
    bt = bt_ref[0]
    for h in range(A_HEADS):
        wh, wl = _split_bf16(jnp.where(mask, w_ref[0, h], 0.0))
        bcol = bt[:, h:h + 1]
        cs = slice(h * A_HEAD_DIM, (h + 1) * A_HEAD_DIM)
        for n in range(ROW_BLK // A_CHUNK):
            rs = slice(n * A_CHUNK, (n + 1) * A_CHUNK)
            s = (jnp.dot(wh, vh[rs, cs], preferred_element_type=F32)
                 + jnp.dot(wh, vl[rs, cs], preferred_element_type=F32)
                 + jnp.dot(wl, vh[rs, cs], preferred_element_type=F32)) + bcol
            a_ref[rs, cs] = (u_ref[rs, cs] * s).astype(BF16)


def _mixer_a(h3, w_stack, bt_stack, ln_g, ln_b, n_prompt_blk, seq_s):
    t = h3.shape[0]
    return pl.pallas_call(
        functools.partial(_mixer_a_kernel, n_prompt_blk, seq_s),
        grid=(t // ROW_BLK,),
        in_specs=[pl.BlockSpec((ROW_BLK, A_WIDTH), lambda i: (i, 0)),
                  pl.BlockSpec((ROW_BLK, A_WIDTH), lambda i: (i, 1)),
                  pl.BlockSpec((1, A_HEADS, A_CHUNK, A_CHUNK), lambda i: (i // n_prompt_blk, 0, 0, 0)),
                  pl.BlockSpec((1, A_CHUNK, A_HEADS), lambda i: (i // n_prompt_blk, 0, 0)),
                  pl.BlockSpec((1, A_WIDTH), lambda i: (0, 0)),
                  pl.BlockSpec((1, A_WIDTH), lambda i: (0, 0))],
        out_specs=[pl.BlockSpec((ROW_BLK, A_WIDTH), lambda i: (i, 0)),
                   pl.BlockSpec((ROW_BLK, A_WIDTH), lambda i: (0, 0))],
        out_shape=[jax.ShapeDtypeStruct((t, 2 * A_WIDTH), BF16),
                   jax.ShapeDtypeStruct((ROW_BLK, A_WIDTH), F32)],
        compiler_params=_params(1),
        name="mixer_a",
    )(h3, h3, w_stack, bt_stack, ln_g.reshape(1, -1), ln_b.reshape(1, -1))


def _attention_heads(q, k, v, neg_dist, mask_bias, sinks_ref, o_ref, row0):
    grp = N_HEADS // N_KV_HEADS
    m_rows = q.shape[0]
    for kv in range(N_KV_HEADS):
        kg = k[:, kv * HEAD_DIM:(kv + 1) * HEAD_DIM]
        vg = v[:, kv * HEAD_DIM:(kv + 1) * HEAD_DIM]
        outs = []
        for j in range(grp):
            h = kv * grp + j
            qh = q[:, h * HEAD_DIM:(h + 1) * HEAD_DIM]
            s = lax.dot_general(qh, kg, (((1,), (1,)), ((), ())), preferred_element_type=F32)
            s = s + (ALIBI_SLOPES[h] * neg_dist + mask_bias)
            sink = sinks_ref[h]
            m = jnp.maximum(jnp.max(s, axis=-1, keepdims=True), sink)
            p = jnp.exp(s - m)
            den = jnp.sum(p, axis=-1, keepdims=True) + jnp.exp(sink - m)
            outs.append(jnp.dot(p.astype(BF16), vg, preferred_element_type=F32) / den)
        for j in range(0, grp, 2):
            c0 = (kv * grp + j) * HEAD_DIM
            o_ref[row0:row0 + m_rows, c0:c0 + 2 * HEAD_DIM] = jnp.concatenate(outs[j:j + 2], axis=1).astype(BF16)


def _attn_prompt_kernel(sinks_ref, q_ref, kv_ref, halo_ref, mix_ref, o_ref):
    del mix_ref
    i = pl.program_id(0)
    kv_all = jnp.concatenate([halo_ref[...], kv_ref[...]], axis=0).astype(BF16)
    half = WINDOW
    n_keys = 2 * WINDOW
    r = lax.broadcasted_iota(I32, (half, n_keys), 0)
    c = lax.broadcasted_iota(I32, (half, n_keys), 1)
    neg_dist = -jnp.abs(r + WINDOW - c).astype(F32)
    qc = r // CHUNK
    kc = c // CHUNK
    band = (kc >= qc) & (kc <= qc + WINDOW // CHUNK)
    for hf in range(ROW_BLK // half):
        visible = band & (c >= jnp.where(i > 0, 0, WINDOW)) if hf == 0 else band
        keys = kv_all[hf * half:hf * half + n_keys]
        q = (q_ref[hf * half:(hf + 1) * half, :] * (HEAD_DIM ** -0.5)).astype(BF16)
        _attention_heads(q, keys[:, :KV_WIDTH], keys[:, KV_WIDTH:], neg_dist,
                         jnp.where(visible, 0.0, NEG_INF), sinks_ref, o_ref, hf * half)


def _attn_prompt(h3, kv, mix, sinks, n_prompt_blk):
    t = h3.shape[0]
    halo_per_blk = ROW_BLK // WINDOW
    return pl.pallas_call(
        _attn_prompt_kernel,
        grid=(n_prompt_blk,),
        in_specs=[pl.BlockSpec(memory_space=pltpu.SMEM),
                  pl.BlockSpec((ROW_BLK, Q_WIDTH), lambda i: (i, 2)),
                  pl.BlockSpec((ROW_BLK, 2 * KV_WIDTH), lambda i: (i, 0)),
                  pl.BlockSpec((WINDOW, 2 * KV_WIDTH), lambda i: (jnp.maximum(i * halo_per_blk - 1, 0), 0)),
                  pl.BlockSpec(memory_space=pl.ANY)],
        out_specs=pl.BlockSpec((ROW_BLK, Q_WIDTH), lambda i: (i, 1)),
        out_shape=jax.ShapeDtypeStruct((t, 2 * Q_WIDTH), BF16),
        input_output_aliases={4: 0},
        compiler_params=_params(1),
        name="attn_prompt",
    )(sinks, h3, kv, kv, mix)


def _attn_sample_kernel(n_streams, seq_s, sinks_ref, q_ref, kv_ref, ck_ref, cv_ref, mix_ref, o_ref):
    del mix_ref
    n_cache = ck_ref.shape[0]
    w_c = n_cache // n_streams
    k_all = jnp.concatenate([ck_ref[...], kv_ref[:, :KV_WIDTH]], axis=0).astype(BF16)
    v_all = jnp.concatenate([cv_ref[...], kv_ref[:, KV_WIDTH:]], axis=0).astype(BF16)
    n_keys = n_cache + ROW_BLK
    r = lax.broadcasted_iota(I32, (ROW_BLK, n_keys), 0)
    c = lax.broadcasted_iota(I32, (ROW_BLK, n_keys), 1)
    is_new = c >= n_cache
    k_stream = jnp.where(is_new, (c - n_cache) // seq_s, c // w_c)
    k_pos = jnp.where(is_new, w_c + (c - n_cache) % seq_s, c % w_c)
    neg_dist = -jnp.abs(w_c + r % seq_s - k_pos).astype(F32)
    mask_bias = jnp.where(k_stream == r // seq_s, 0.0, NEG_INF)
    q = (q_ref[...] * (HEAD_DIM ** -0.5)).astype(BF16)
    _attention_heads(q, k_all, v_all, neg_dist, mask_bias, sinks_ref, o_ref, 0)


def _attn_sample(h3, kv, mix, cache_k2d, cache_v2d, sinks, n_prompt_blk, n_streams, seq_s):
    t = h3.shape[0]
    full = lambda i: (0, 0)
    return pl.pallas_call(
        functools.partial(_attn_sample_kernel, n_streams, seq_s),
        grid=(1,),
        in_specs=[pl.BlockSpec(memory_space=pltpu.SMEM),
                  pl.BlockSpec((ROW_BLK, Q_WIDTH), lambda i: (n_prompt_blk, 2)),
                  pl.BlockSpec((ROW_BLK, 2 * KV_WIDTH), lambda i: (n_prompt_blk, 0)),
                  pl.BlockSpec(cache_k2d.shape, full),
                  pl.BlockSpec(cache_v2d.shape, full),
                  pl.BlockSpec(memory_space=pl.ANY)],
        out_specs=pl.BlockSpec((ROW_BLK, Q_WIDTH), lambda i: (n_prompt_blk, 1)),
        out_shape=jax.ShapeDtypeStruct((t, 2 * Q_WIDTH), BF16),
        input_output_aliases={5: 0},
        compiler_params=_params(1),
        name="attn_sample",
    )(sinks, h3, kv, cache_k2d, cache_v2d, mix)


def _conv_kernel(n_prompt_blk, n_streams, seq_s, z_ref, halo_ref, go_ref, st_ref, cw_ref, o_ref):
    i = pl.program_id(0)
    z = z_ref[...]
    rows = lax.broadcasted_iota(I32, (ROW_BLK, 1), 0)
    zm1 = pltpu.roll(z, 1, 0)
    zm2 = pltpu.roll(z, 2, 0)

    def finish(zm1, zm2):
        conv = cw_ref[0:1, :] * zm2 + cw_ref[1:2, :] * zm1 + cw_ref[2:3, :] * z
        o_ref[...] = (go_ref[...] * conv).astype(BF16)

    @pl.when(i < n_prompt_blk)
    def _():
        live = jnp.where(i > 0, 1.0, 0.0)
        p1 = halo_ref[7:8, :] * live
        p2 = halo_ref[6:7, :] * live
        finish(jnp.where(rows == 0, p1, zm1),
               jnp.where(rows == 0, p2, jnp.where(rows == 1, p1, zm2)))

    @pl.when(i == n_prompt_blk)
    def _():
        a, b = zm1, zm2
        for s in range(n_streams):
            s0 = st_ref[2 * s:2 * s + 1, :]
            s1 = st_ref[2 * s + 1:2 * s + 2, :]
            a = jnp.where(rows == s * seq_s, s1, a)
            b = jnp.where(rows == s * seq_s, s0, jnp.where(rows == s * seq_s + 1, s1, b))
        finish(a, b)


def _gated_conv(z, gate_out, state2d, conv_w, n_prompt_blk, n_streams, seq_s):
    t = z.shape[0]
    halo_per_blk = ROW_BLK // 8
    return pl.pallas_call(
        functools.partial(_conv_kernel, n_prompt_blk, n_streams, seq_s),
        grid=(t // ROW_BLK,),
        in_specs=[pl.BlockSpec((ROW_BLK, D_MODEL), lambda i: (i, 0)),
                  pl.BlockSpec((8, D_MODEL), lambda i: (jnp.maximum(i * halo_per_blk - 1, 0), 0)),
                  pl.BlockSpec((ROW_BLK, D_MODEL), lambda i: (i, 0)),
                  pl.BlockSpec(state2d.shape, lambda i: (0, 0)),
                  pl.BlockSpec(conv_w.shape, lambda i: (0, 0))],
        out_specs=pl.BlockSpec((ROW_BLK, D_MODEL), lambda i: (i, 0)),
        out_shape=jax.ShapeDtypeStruct((t, D_MODEL), BF16),
        compiler_params=_params(1),
        name="gated_conv",
    )(z, z, gate_out, state2d, conv_w)


def _split_bf16(x):
    hi = x.astype(BF16)
    lo = (x - hi.astype(F32)).astype(BF16)
    return hi, lo


def _router_kernel(n_blk, x_ref, w_ref, br_ref, ri_ref, gate_ref, cnt_ref, carry_ref, wsplit_ref):
    i = pl.program_id(0)

    @pl.when(i == 0)
    def _():
        carry_ref[...] = jnp.zeros_like(carry_ref)
        wh, wl = _split_bf16(w_ref[...])
        wsplit_ref[0] = wh
        wsplit_ref[1] = wl

    xh, xl = _split_bf16(x_ref[...])
    wh, wl = wsplit_ref[0], wsplit_ref[1]
    logits = (jnp.dot(xh, wh, preferred_element_type=F32) + jnp.dot(xh, wl, preferred_element_type=F32)
              + jnp.dot(xl, wh, preferred_element_type=F32))
    lt = logits.T[:N_EXPERTS]
    ex = jnp.exp(lt - jnp.max(lt, axis=0, keepdims=True))
    scores = ex / jnp.sum(ex, axis=0, keepdims=True)
    biased = scores + br_ref[...]
    sc = [scores[e:e + 1] for e in range(N_EXPERTS)]
    bs = [biased[e:e + 1] for e in range(N_EXPERTS)]

    gscore = []
    for g in range(N_GROUPS):
        a, b, c, d = bs[4 * g:4 * g + 4]
        hi1, lo1 = jnp.maximum(a, b), jnp.minimum(a, b)
        hi2, lo2 = jnp.maximum(c, d), jnp.minimum(c, d)
        gscore.append(jnp.maximum(hi1, hi2) + jnp.maximum(jnp.minimum(hi1, hi2), jnp.maximum(lo1, lo2)))
    best, grp = gscore[0], jnp.zeros((1, ROW_BLK), I32)
    for g in range(1, N_GROUPS):
        better = gscore[g] > best
        grp = jnp.where(better, g, grp)
        best = jnp.where(better, gscore[g], best)

    def pick(rows, idx, n):
        out = rows[0]
        for j in range(1, n):
            out = jnp.where(idx == j, rows[j], out)
        return out

    in_b = [pick([bs[4 * g + j] for g in range(N_GROUPS)], grp, N_GROUPS) for j in range(EXPERTS_PER_GROUP)]
    in_s = [pick([sc[4 * g + j] for g in range(N_GROUPS)], grp, N_GROUPS) for j in range(EXPERTS_PER_GROUP)]
    v1, l1 = in_b[0], jnp.zeros((1, ROW_BLK), I32)
    for j in range(1, EXPERTS_PER_GROUP):
        better = in_b[j] > v1
        l1 = jnp.where(better, j, l1)
        v1 = jnp.where(better, in_b[j], v1)
    v2, l2 = jnp.full((1, ROW_BLK), -jnp.inf, F32), jnp.zeros((1, ROW_BLK), I32)
    for j in range(EXPERTS_PER_GROUP):
        better = (l1 != j) & (in_b[j] > v2)
        l2 = jnp.where(better, j, l2)
        v2 = jnp.where(better, in_b[j], v2)
    gate1 = pick(in_s, l1, EXPERTS_PER_GROUP)
    gate2 = pick(in_s, l2, EXPERTS_PER_GROUP)
    gsum = gate1 + gate2
    gate_ref[0:1, :] = gate1 / gsum
    gate_ref[1:2, :] = gate2 / gsum
    e1 = grp * EXPERTS_PER_GROUP + l1
    e2 = grp * EXPERTS_PER_GROUP + l2
    ri_ref[0:1, :] = e1
    ri_ref[1:2, :] = e2

    expert = lax.broadcasted_iota(I32, (N_EXPERTS, ROW_BLK), 0)
    oh1 = (expert == e1).astype(F32)
    oh2 = (expert == e2).astype(F32)
    r = lax.broadcasted_iota(I32, (ROW_BLK, ROW_BLK), 0)
    c = lax.broadcasted_iota(I32, (ROW_BLK, ROW_BLK), 1)
    before = (r < c).astype(BF16)
    cs = jnp.dot(jnp.concatenate([oh1, oh2], axis=0).astype(BF16), before, preferred_element_type=F32)
    tot1 = jnp.sum(oh1, axis=1, keepdims=True)
    tot2 = jnp.sum(oh2, axis=1, keepdims=True)
    carry = carry_ref[...][:, 0:1]
    ri_ref[2:3, :] = jnp.sum(oh1 * (carry + cs[:N_EXPERTS]), axis=0, keepdims=True).astype(I32)
    ri_ref[3:4, :] = jnp.sum(oh2 * (carry + tot1 + cs[N_EXPERTS:]), axis=0, keepdims=True).astype(I32)
    carry_ref[...] = carry_ref[...] + tot1 + tot2

    @pl.when(i == n_blk - 1)
    def _():
        cnt_ref[...] = carry_ref[...]


def _router(x, w_pad, b_col):
    t = x.shape[0]
    n_blk = t // ROW_BLK
    return pl.pallas_call(
        functools.partial(_router_kernel, n_blk),
        grid=(n_blk,),
        in_specs=[pl.BlockSpec((ROW_BLK, D_MODEL), lambda i: (i, 0)),
                  pl.BlockSpec((D_MODEL, LANES), lambda i: (0, 0)),
                  pl.BlockSpec((N_EXPERTS, 1), lambda i: (0, 0))],
        out_specs=[pl.BlockSpec((4, ROW_BLK), lambda i: (0, i)),
                   pl.BlockSpec((2, ROW_BLK), lambda i: (0, i)),
                   pl.BlockSpec((N_EXPERTS, LANES), lambda i: (0, 0))],
        out_shape=[jax.ShapeDtypeStruct((4, t), I32),
                   jax.ShapeDtypeStruct((2, t), F32),
                   jax.ShapeDtypeStruct((N_EXPERTS, LANES), F32)],
        scratch_shapes=[pltpu.VMEM((N_EXPERTS, LANES), F32), pltpu.VMEM((2, D_MODEL, LANES), BF16)],
        compiler_params=_params(1),
        name="router",
    )(x, w_pad, b_col)


def _dispatch_kernel(pos_ref, meta_ref, x_ref, xs_ref, sem):
    i = pl.program_id(0)
    base = i * ROW_BLK

    def row_copy(src_row, dst_row):
        return pltpu.make_async_copy(x_ref.at[pl.ds(src_row, 1)], xs_ref.at[pl.ds(dst_row, 1)], sem)

    def group(g, carry):
        tile_rows = x_ref.at[pl.ds(pl.multiple_of(g * SUBLANES, SUBLANES), SUBLANES)]
        for j in range(SUBLANES):
            t = base + g * SUBLANES + j
            for k in range(2):
                pltpu.make_async_copy(tile_rows.at[pl.ds(j, 1)], xs_ref.at[pl.ds(pos_ref[2 * t + k], 1)],
                                      sem).start(priority=k)
        return carry

    lax.fori_loop(0, ROW_BLK // SUBLANES, group, 0)

    @pl.when(i == 0)
    def _():
        for e in range(N_EXPERTS):
            cnt, start, padded = meta_ref[e], meta_ref[N_EXPERTS + e], meta_ref[2 * N_EXPERTS + e]

            def fill(ri, carry):
                row_copy(0, start + ri).start()
                return carry

            def drain(ri, carry):
                row_copy(0, 0).wait()
                return carry

            lax.fori_loop(cnt, padded, fill, 0)
            lax.fori_loop(cnt, padded, drain, 0)

    for _ in range(2):
        pltpu.make_async_copy(x_ref, xs_ref.at[pl.ds(0, ROW_BLK)], sem).wait()


def _dispatch(pos_flat, meta, x, n_rows):
    t = x.shape[0]
    grid_spec = pltpu.PrefetchScalarGridSpec(
        num_scalar_prefetch=2,
        grid=(t // ROW_BLK,),
        in_specs=[pl.BlockSpec((ROW_BLK, D_MODEL), lambda i, pos, meta: (i, 0))],
        out_specs=pl.BlockSpec(memory_space=pl.ANY),
        scratch_shapes=[pltpu.SemaphoreType.DMA(())],
    )
    return pl.pallas_call(
        _dispatch_kernel,
        grid_spec=grid_spec,
        out_shape=jax.ShapeDtypeStruct((n_rows, D_MODEL), x.dtype),
        compiler_params=_params(1),
        name="dispatch",
    )(pos_flat, meta, x)


def _expert_mlp_kernel(li, first_ref, nt_ref, row0_ref, slot_ref, next_ref, xs_ref, wg_ref, wu_ref, wd_ref,
                       ys_ref, wgb, wub, wdb, sg, su, sd, xstage, ostage, w_sem, x_sem, y_sem):
    e = pl.program_id(0)
    nt = nt_ref[e]
    row0 = row0_ref[e]
    slot = slot_ref[e]
    e_next = next_ref[e]
    prefetch = e_next >= 0
    w_parts = ((wg_ref, sg, wgb, D_MODEL // W_CHUNKS),
               (wu_ref, su, wub, D_MODEL // W_CHUNKS),
               (wd_ref, sd, wdb, D_EXPERT // W_CHUNKS))

    def w_copy(k, expert, c):
        src, stage, _, rows = w_parts[k]
        s = c % W_STAGES
        return pltpu.make_async_copy(src.at[li, expert, pl.ds(pl.multiple_of(c * rows, rows), rows)],
                                     stage.at[s], w_sem.at[k, s])

    def w_start(expert):
        for c in range(W_STAGES - 1):
            for k in range(3):
                w_copy(k, expert, c).start()

    def w_chunk(expert, c, dst_slot):
        for k in range(3):
            w_copy(k, expert, c).wait()

        @pl.when(c + W_STAGES - 1 < W_CHUNKS)
        def _():
            for k in range(3):
                w_copy(k, expert, c + W_STAGES - 1).start()

        for _, stage, dst, rows in w_parts:
            dst[dst_slot, pl.ds(pl.multiple_of(c * rows, rows), rows), :] = stage[c % W_STAGES].astype(BF16)

    def hbm_rows(r):
        return pl.ds(pl.multiple_of(row0 + r * MOE_ROWS, MOE_ROWS), MOE_ROWS)

    def x_load(r, s):
        return pltpu.make_async_copy(xs_ref.at[hbm_rows(r)], xstage.at[s], x_sem.at[s])

    def y_store(r, s):
        return pltpu.make_async_copy(ostage.at[s], ys_ref.at[hbm_rows(r)], y_sem.at[s])

    def tile(r):
        s = r % 2
        x_load(r, s).wait()

        @pl.when(r + 1 < nt)
        def _():
            x_load(r + 1, 1 - s).start(priority=ROW_TILE_DMA_PRIORITY)

        xb = xstage[s].astype(BF16)
        g = jnp.dot(xb, wgb[slot], preferred_element_type=F32)
        u = jnp.dot(xb, wub[slot], preferred_element_type=F32)
        h = (g * jax.nn.sigmoid(g) * u).astype(BF16)
        out = jnp.dot(h, wdb[slot], preferred_element_type=F32)

        @pl.when(r >= 2)
        def _():
            y_store(r - 2, s).wait()

        ostage[s] = out
        y_store(r, s).start(priority=ROW_TILE_DMA_PRIORITY)

    @pl.when(e == first_ref[0])
    def _():
        w_start(e)

        def own(c, carry):
            w_chunk(e, c, slot)
            return carry

        lax.fori_loop(0, W_CHUNKS, own, 0)

    @pl.when(nt > 0)
    def _():
        @pl.when(prefetch)
        def _():
            w_start(e_next)

        x_load(0, 0).start(priority=ROW_TILE_DMA_PRIORITY)

        def step(r, carry):
            @pl.when(r < nt)
            def _():
                tile(r)

            @pl.when(prefetch & (r < W_CHUNKS))
            def _():
                w_chunk(e_next, r, 1 - slot)

            return carry

        lax.fori_loop(0, jnp.maximum(nt, jnp.where(prefetch, W_CHUNKS, 0)), step, 0)
        y_store(0, (nt - 1) % 2).wait()

        @pl.when(nt >= 2)
        def _():
            y_store(0, nt % 2).wait()


def _expert_mlp(first, tiles_e, starts, slot_e, next_e, xs, w_gate, w_up, w_down, li):
    n_rows = xs.shape[0]
    qg, qd = D_MODEL // W_CHUNKS, D_EXPERT // W_CHUNKS
    grid_spec = pltpu.PrefetchScalarGridSpec(
        num_scalar_prefetch=5,
        grid=(N_EXPERTS,),
        in_specs=[pl.BlockSpec(memory_space=pl.ANY)] * 4,
        out_specs=pl.BlockSpec(memory_space=pl.ANY),
        scratch_shapes=[pltpu.VMEM((2, D_MODEL, D_EXPERT), BF16),
                        pltpu.VMEM((2, D_MODEL, D_EXPERT), BF16),
                        pltpu.VMEM((2, D_EXPERT, D_MODEL), BF16),
                        pltpu.VMEM((W_STAGES, qg, D_EXPERT), F32),
                        pltpu.VMEM((W_STAGES, qg, D_EXPERT), F32),
                        pltpu.VMEM((W_STAGES, qd, D_MODEL), F32),
                        pltpu.VMEM((2, MOE_ROWS, D_MODEL), xs.dtype),
                        pltpu.VMEM((2, MOE_ROWS, D_MODEL), F32),
                        pltpu.SemaphoreType.DMA((3, W_STAGES)),
                        pltpu.SemaphoreType.DMA((2,)),
                        pltpu.SemaphoreType.DMA((2,))],
    )
    return pl.pallas_call(
        functools.partial(_expert_mlp_kernel, li),
        grid_spec=grid_spec,
        out_shape=jax.ShapeDtypeStruct((n_rows, D_MODEL), F32),
        compiler_params=_params(1),
        name="expert_mlp",
    )(first, tiles_e, starts, slot_e, next_e, xs, w_gate, w_up, w_down)


def _combine_kernel(n_prompt_blk, pos_ref, ys_ref, x_ref, gate_ref, g_ref, b_ref, oa_ref, ob_ref, buf_ref, sem):
    i = pl.program_id(0)
    slot = i % 2

    def gather(blk, s):
        def group(g, carry):
            for k in range(2):
                tile_rows = buf_ref.at[s, k, pl.ds(pl.multiple_of(g * SUBLANES, SUBLANES), SUBLANES)]
                for j in range(SUBLANES):
                    t = blk * ROW_BLK + g * SUBLANES + j
                    pltpu.make_async_copy(ys_ref.at[pl.ds(pos_ref[2 * t + k], 1)],
                                          tile_rows.at[pl.ds(j, 1)], sem.at[s]).start(priority=k)
            return carry

        lax.fori_loop(0, ROW_BLK // SUBLANES, group, 0)

    @pl.when(i == 0)
    def _():
        gather(0, 0)

    @pl.when(i + 1 < pl.num_programs(0))
    def _():
        gather(i + 1, 1 - slot)

    for k in range(2):
        pltpu.make_async_copy(ys_ref.at[pl.ds(0, ROW_BLK)], buf_ref.at[slot, k], sem.at[slot]).wait()
    gates = gate_ref[...]
    y = ALPHA * x_ref[...] + gates[:, 0:1] * buf_ref[slot, 0] + gates[:, 1:2] * buf_ref[slot, 1]
    y = _layer_norm(y, g_ref[...], b_ref[...])
    if n_prompt_blk is None:
        oa_ref[...] = y
        ob_ref[...] = y.astype(BF16)
    else:
        @pl.when(i < n_prompt_blk)
        def _():
            oa_ref[...] = y

        @pl.when(i == n_prompt_blk)
        def _():
            ob_ref[...] = y


def _combine_ln(pos_flat, ys, x, gates, g, b, n_prompt_blk=None):
    t = x.shape[0]
    row = lambda i, pos: (i, 0)
    vec = lambda i, pos: (0, 0)
    if n_prompt_blk is None:
        out_specs = [pl.BlockSpec((ROW_BLK, D_MODEL), row), pl.BlockSpec((ROW_BLK, D_MODEL), row)]
        out_shape = [jax.ShapeDtypeStruct((t, D_MODEL), F32), jax.ShapeDtypeStruct((t, D_MODEL), BF16)]
    else:
        out_specs = [pl.BlockSpec((ROW_BLK, D_MODEL), lambda i, pos: (jnp.minimum(i, n_prompt_blk - 1), 0)),
                     pl.BlockSpec((ROW_BLK, D_MODEL), vec)]
        out_shape = [jax.ShapeDtypeStruct((n_prompt_blk * ROW_BLK, D_MODEL), F32),
                     jax.ShapeDtypeStruct((t - n_prompt_blk * ROW_BLK, D_MODEL), F32)]
    grid_spec = pltpu.PrefetchScalarGridSpec(
        num_scalar_prefetch=1,
        grid=(t // ROW_BLK,),
        in_specs=[pl.BlockSpec(memory_space=pl.ANY),
                  pl.BlockSpec((ROW_BLK, D_MODEL), row),
                  pl.BlockSpec((ROW_BLK, 2), row),
                  pl.BlockSpec((1, D_MODEL), vec),
                  pl.BlockSpec((1, D_MODEL), vec)],
        out_specs=out_specs,
        scratch_shapes=[pltpu.VMEM((2, 2, ROW_BLK, D_MODEL), F32), pltpu.SemaphoreType.DMA((2,))],
    )
    return pl.pallas_call(
        functools.partial(_combine_kernel, n_prompt_blk),
        grid_spec=grid_spec,
        out_shape=out_shape,
        compiler_params=_params(1),
        name="combine_ln",
    )(pos_flat, ys, x, gates, g.reshape(1, -1), b.reshape(1, -1))


def _moe_layer(x, w_pad, br_pad, w_gate, w_up, w_down, li, ln_g, ln_b, n_prompt_blk=None):
    t = x.shape[0]
    n_tiles = -(-(2 * t + N_EXPERTS * (MOE_ROWS - 1)) // MOE_ROWS)
    ri, gates, cnt = _router(x, w_pad, br_pad)
    gates = gates.T
    counts = cnt[:, 0].astype(I32)
    tiles_e = (counts + MOE_ROWS - 1) // MOE_ROWS
    padded = tiles_e * MOE_ROWS
    starts = (jnp.cumsum(padded) - padded).astype(I32)
    onehot = ri[0:2, :, None] == jnp.arange(N_EXPERTS, dtype=I32)
    pos = jnp.sum(jnp.where(onehot, starts, 0), axis=-1) + ri[2:4]
    pos_flat = pos.T.reshape(-1)
    meta = jnp.concatenate([counts, starts, padded]).astype(I32)
    ids = jnp.arange(N_EXPERTS, dtype=I32)
    live = tiles_e > 0
    slot_e = ((jnp.cumsum(live) - live) % 2).astype(I32)
    later = live[None, :] & (ids[None, :] > ids[:, None])
    next_e = jnp.min(jnp.where(later, ids[None, :], N_EXPERTS), axis=1)
    next_e = jnp.where(next_e < N_EXPERTS, next_e, -1).astype(I32)
    first = jnp.min(jnp.where(live, ids, N_EXPERTS - 1)).astype(I32).reshape(1)

    xs = _dispatch(pos_flat, meta, x, n_tiles * MOE_ROWS)
    ys = _expert_mlp(first, tiles_e.astype(I32), starts, slot_e, next_e, xs, w_gate, w_up, w_down, li)
    return _combine_ln(pos_flat, ys, x, gates, ln_g, ln_b, n_prompt_blk)


def kernel(x_prompt, x_sample, cache_k, cache_v, state_conv, w_in_even, ln_v_g, ln_v_b, w_spatial, b_spatial, sinks, w_out_even, w_in_odd, conv_w, w_out_odd, ln_mix_g, ln_mix_b, ln_ffn_g, ln_ffn_b, w_router, b_router, w_gate, w_up, w_down):
    n_batch, seq_p, _ = x_prompt.shape
    n_streams, seq_s, _ = x_sample.shape
    n_p = n_batch * seq_p
    n_s = n_streams * seq_s
    assert n_batch == 1 and n_s == ROW_BLK and n_p % ROW_BLK == 0 and (n_p + n_s) % MM_ROWS == 0
    assert seq_s & (seq_s - 1) == 0 and A_CHUNK % seq_s == 0 and cache_k.shape[2] == WINDOW
    n_prompt_blk = n_p // ROW_BLK
    win_p = min(WINDOW, seq_p)

    x = jnp.concatenate([x_prompt.reshape(n_p, D_MODEL), x_sample.reshape(n_s, D_MODEL)], axis=0)
    xb = x
    w_pad = jnp.pad(w_router, ((0, 0), (0, LANES - N_EXPERTS)))
    br_pad = b_router.reshape(N_EXPERTS, 1)
    reps = A_CHUNK // seq_s

    k_p, v_p, k_s, v_s, vg_s, c_p, c_s = [], [], [], [], [], [], []
    for layer in range(DEPTH):
        i = layer // 2
        if layer % 2 == 0:
            h3, kv = _even_in_proj(xb, w_in_even, i)
            w_stack = jnp.stack([w_spatial[i], jnp.tile(w_spatial[i][:, :seq_s, :seq_s], (1, reps, reps))])
            bt_stack = jnp.stack([b_spatial[i].T, jnp.tile(b_spatial[i][:, :seq_s], (1, reps)).T])
            mix, vn_s = _mixer_a(h3, w_stack, bt_stack, ln_v_g[i], ln_v_b[i], n_prompt_blk, seq_s)
            mix = _attn_prompt(h3, kv, mix, sinks[i], n_prompt_blk)
            mix = _attn_sample(h3, kv, mix, cache_k[i].reshape(n_streams * WINDOW, KV_WIDTH),
                               cache_v[i].reshape(n_streams * WINDOW, KV_WIDTH), sinks[i],
                               n_prompt_blk, n_streams, seq_s)
            x, xb = _out_proj_ln(mix, w_out_even, i, x, ln_mix_g[layer], ln_mix_b[layer])
            k_p.append(kv[n_p - win_p:n_p, :KV_WIDTH].reshape(n_batch, win_p, N_KV_HEADS, HEAD_DIM))
            v_p.append(kv[n_p - win_p:n_p, KV_WIDTH:].reshape(n_batch, win_p, N_KV_HEADS, HEAD_DIM))
            k_s.append(kv[n_p:, :KV_WIDTH].reshape(n_streams, seq_s, N_KV_HEADS, HEAD_DIM))
            v_s.append(kv[n_p:, KV_WIDTH:].reshape(n_streams, seq_s, N_KV_HEADS, HEAD_DIM))
            vg_s.append(vn_s.reshape(n_streams, seq_s, A_WIDTH))
        else:
            gate_out, z = _odd_in_proj(xb, w_in_odd, i)
            g = _gated_conv(z, gate_out, state_conv[i].reshape(n_streams * (CONV_WIDTH - 1), D_MODEL),
                            conv_w[i], n_prompt_blk, n_streams, seq_s)
            x, xb = _out_proj_ln(g, w_out_odd, i, x, ln_mix_g[layer], ln_mix_b[layer])
            c_p.append(z[n_p - (CONV_WIDTH - 1):n_p].reshape(n_batch, CONV_WIDTH - 1, D_MODEL))
            c_s.append(z[n_p:].reshape(n_streams, seq_s, D_MODEL)[:, seq_s - (CONV_WIDTH - 1):])
        x, xb = _moe_layer(x, w_pad, br_pad, w_gate, w_up, w_down, layer,
                           ln_ffn_g[layer], ln_ffn_b[layer],
                           n_prompt_blk if layer == DEPTH - 1 else None)
    return (x.reshape(n_batch, seq_p, D_MODEL), xb.reshape(n_streams, seq_s, D_MODEL),
            jnp.stack(k_p), jnp.stack(v_p), jnp.stack(k_s), jnp.stack(v_s),
            jnp.stack(vg_s), jnp.stack(c_p), jnp.stack(c_s))
```

```python
import functools

import jax
import jax.numpy as jnp
from jax import lax
from jax.experimental import pallas as pl
from jax.experimental.pallas import tpu as pltpu

F32 = jnp.float32
BF16 = jnp.bfloat16
I32 = jnp.int32

D_MODEL = 2048
DEPTH = 4
CHUNK = 64
A_HEADS = 8
A_HEAD_DIM = 128
A_WIDTH = A_HEADS * A_HEAD_DIM
A_CHUNK = 128
N_HEADS = 16
N_KV_HEADS = 2
HEAD_DIM = 64
Q_WIDTH = N_HEADS * HEAD_DIM
KV_WIDTH = N_KV_HEADS * HEAD_DIM
WINDOW = 128
CONV_WIDTH = 3
N_EXPERTS = 16
N_GROUPS = 4
EXPERTS_PER_GROUP = N_EXPERTS // N_GROUPS
D_EXPERT = 1024
ALPHA = (2 * DEPTH) ** 0.25
LN_EPS = 1e-5
NEG_INF = -1e30
ALIBI_SLOPES = tuple(2.0 ** (-8.0 * h / N_HEADS) for h in range(1, N_HEADS + 1))

LANES = 128
SUBLANES = 8
ROW_BLK = 256
MM_ROWS = 768
OUT_ROWS = 384
MOE_ROWS = 256
W_CHUNKS = 4
W_STAGES = 2
VMEM_LIMIT = 56 * 1024 * 1024


def _params(n_axes, **kw):
    return pltpu.CompilerParams(dimension_semantics=("arbitrary",) * n_axes,
                                vmem_limit_bytes=VMEM_LIMIT, **kw)


def _layer_norm(y, g, b):
    mu = jnp.mean(y, axis=-1, keepdims=True)
    yc = y - mu
    var = jnp.mean(yc * yc, axis=-1, keepdims=True)
    return yc * lax.rsqrt(var + LN_EPS) * g + b


def _odd_in_kernel(x_ref, wo_ref, wi_ref, wh_ref, go_ref, z_ref, wb_ref):
    @pl.when(pl.program_id(1) == 0)
    def _():
        wb_ref[0] = wo_ref[0].astype(BF16)
        wb_ref[1] = wi_ref[0].astype(BF16)
        wb_ref[2] = wh_ref[0].astype(BF16)

    x = x_ref[...]
    go_ref[...] = jnp.dot(x, wb_ref[0], preferred_element_type=F32)
    gate_in = jnp.dot(x, wb_ref[1], preferred_element_type=F32)
    h = jnp.dot(x, wb_ref[2], preferred_element_type=F32)
    z_ref[...] = gate_in * h


def _odd_in_proj(xb, w_in, li):
    t, k = xb.shape
    tn = 512
    nt = D_MODEL // tn
    out = jax.ShapeDtypeStruct((t, D_MODEL), F32)
    return pl.pallas_call(
        _odd_in_kernel,
        grid=(nt, t // MM_ROWS),
        in_specs=[pl.BlockSpec((MM_ROWS, k), lambda j, i: (i, 0)),
                  pl.BlockSpec((1, k, tn), lambda j, i: (li, 0, j)),
                  pl.BlockSpec((1, k, tn), lambda j, i: (li, 0, nt + j)),
                  pl.BlockSpec((1, k, tn), lambda j, i: (li, 0, 2 * nt + j))],
        out_specs=[pl.BlockSpec((MM_ROWS, tn), lambda j, i: (i, j)),
                   pl.BlockSpec((MM_ROWS, tn), lambda j, i: (i, j))],
        out_shape=[out, out],
        scratch_shapes=[pltpu.VMEM((3, k, tn), BF16)],
        compiler_params=_params(2),
        name="odd_in_proj",
    )(xb, w_in, w_in, w_in)


def _load_weight_bf16(w_ref, li, wb_ref, stage_ref, sem):
    kdim = wb_ref.shape[0]
    rows = stage_ref.shape[1]
    n_chunks = kdim // rows

    @pl.when(pl.program_id(0) == 0)
    def _():
        def chunk_copy(c):
            return pltpu.make_async_copy(w_ref.at[li, pl.ds(c * rows, rows)], stage_ref.at[c % 2], sem.at[c % 2])

        chunk_copy(0).start()
        for c in range(n_chunks):
            if c + 1 < n_chunks:
                chunk_copy(c + 1).start()
            chunk_copy(c).wait()
            wb_ref[c * rows:(c + 1) * rows, :] = stage_ref[c % 2].astype(BF16)


def _even_in_kernel(li, x_ref, w_ref, h_ref, kv_ref, wb_ref, stage_ref, sem):
    _load_weight_bf16(w_ref, li, wb_ref, stage_ref, sem)
    out = jnp.dot(x_ref[...].astype(BF16), wb_ref[...], preferred_element_type=F32)
    n_h = h_ref.shape[1]
    h_ref[...] = out[:, :n_h]
    kv_ref[...] = out[:, n_h:]


def _even_in_proj(xb, w_in, li):
    t, k = xb.shape
    n_all = w_in.shape[2]
    n_kv = 2 * KV_WIDTH
    row = lambda i: (i, 0)
    return pl.pallas_call(
        functools.partial(_even_in_kernel, li),
        grid=(t // OUT_ROWS,),
        in_specs=[pl.BlockSpec((OUT_ROWS, k), row), pl.BlockSpec(memory_space=pl.ANY)],
        out_specs=[pl.BlockSpec((OUT_ROWS, n_all - n_kv), row), pl.BlockSpec((OUT_ROWS, n_kv), row)],
        out_shape=[jax.ShapeDtypeStruct((t, n_all - n_kv), F32), jax.ShapeDtypeStruct((t, n_kv), F32)],
        scratch_shapes=[pltpu.VMEM((k, n_all), BF16),
                        pltpu.VMEM((2, k // 8, n_all), F32),
                        pltpu.SemaphoreType.DMA((2,))],
        compiler_params=_params(1),
        name="even_in_proj",
    )(xb, w_in)


def _out_proj_kernel(li, l_ref, w_ref, r_ref, g_ref, b_ref, of_ref, ob_ref, wb_ref, stage_ref, sem):
    _load_weight_bf16(w_ref, li, wb_ref, stage_ref, sem)
    y = ALPHA * r_ref[...] + jnp.dot(l_ref[...], wb_ref[...], preferred_element_type=F32)
    y = _layer_norm(y, g_ref[...], b_ref[...])
    of_ref[...] = y
    ob_ref[...] = y.astype(BF16)


def _out_proj_ln(lhs, w, li, resid, g, b):
    t, kdim = lhs.shape
    row = lambda i: (i, 0)
    vec = lambda i: (0, 0)
    return pl.pallas_call(
        functools.partial(_out_proj_kernel, li),
        grid=(t // OUT_ROWS,),
        in_specs=[pl.BlockSpec((OUT_ROWS, kdim), row),
                  pl.BlockSpec(memory_space=pl.ANY),
                  pl.BlockSpec((OUT_ROWS, D_MODEL), row),
                  pl.BlockSpec((1, D_MODEL), vec),
                  pl.BlockSpec((1, D_MODEL), vec)],
        out_specs=[pl.BlockSpec((OUT_ROWS, D_MODEL), row),
                   pl.BlockSpec((OUT_ROWS, D_MODEL), row)],
        out_shape=[jax.ShapeDtypeStruct((t, D_MODEL), F32),
                   jax.ShapeDtypeStruct((t, D_MODEL), BF16)],
        scratch_shapes=[pltpu.VMEM((kdim, D_MODEL), BF16),
                        pltpu.VMEM((2, kdim // 4, D_MODEL), F32),
                        pltpu.SemaphoreType.DMA((2,))],
        compiler_params=_params(1),
        name="out_proj_ln",
    )(lhs, w, resid, g.reshape(1, -1), b.reshape(1, -1))


def _mixer_a_kernel(n_prompt_blk, seq_s, u_ref, v_ref, w_ref, bt_ref, g_ref, b_ref, a_ref, vn_ref):
    i = pl.program_id(0)
    is_sample = i == n_prompt_blk
    vn = _layer_norm(v_ref[...], g_ref[...], b_ref[...])

    @pl.when(is_sample)
    def _():
        vn_ref[...] = vn

    shift = jnp.where(is_sample, seq_s.bit_length() - 1, A_CHUNK.bit_length() - 1)
    low = jnp.where(is_sample, seq_s - 1, A_CHUNK - 1)
    r = lax.broadcasted_iota(I32, (A_CHUNK, A_CHUNK), 0)
    c = lax.broadcasted_iota(I32, (A_CHUNK, A_CHUNK), 1)
    mask = ((r >> shift) == (c >> shift)) & ((c & low) <= (r & low))
    vh, vl = _split_bf16(vn)
    bt = bt_ref[0]
    for h in range(A_HEADS):
        wh, wl = _split_bf16(jnp.where(mask, w_ref[0, h], 0.0))
        bcol = bt[:, h:h + 1]
        cs = slice(h * A_HEAD_DIM, (h + 1) * A_HEAD_DIM)
        for n in range(ROW_BLK // A_CHUNK):
            rs = slice(n * A_CHUNK, (n + 1) * A_CHUNK)
            s = (jnp.dot(wh, vh[rs, cs], preferred_element_type=F32)
                 + jnp.dot(wh, vl[rs, cs], preferred_element_type=F32)
                 + jnp.dot(wl, vh[rs, cs], preferred_element_type=F32)) + bcol
            a_ref[rs, cs] = (u_ref[rs, cs] * s).astype(BF16)


def _mixer_a(h3, w_stack, bt_stack, ln_g, ln_b, n_prompt_blk, seq_s):
    t = h3.shape[0]
    return pl.pallas_call(
        functools.partial(_mixer_a_kernel, n_prompt_blk, seq_s),
        grid=(t // ROW_BLK,),
        in_specs=[pl.BlockSpec((ROW_BLK, A_WIDTH), lambda i: (i, 0)),
                  pl.BlockSpec((ROW_BLK, A_WIDTH), lambda i: (i, 1)),
                  pl.BlockSpec((1, A_HEADS, A_CHUNK, A_CHUNK), lambda i: (i // n_prompt_blk, 0, 0, 0)),
                  pl.BlockSpec((1, A_CHUNK, A_HEADS), lambda i: (i // n_prompt_blk, 0, 0)),
                  pl.BlockSpec((1, A_WIDTH), lambda i: (0, 0)),
                  pl.BlockSpec((1, A_WIDTH), lambda i: (0, 0))],
        out_specs=[pl.BlockSpec((ROW_BLK, A_WIDTH), lambda i: (i, 0)),
                   pl.BlockSpec((ROW_BLK, A_WIDTH), lambda i: (0, 0))],
        out_shape=[jax.ShapeDtypeStruct((t, 2 * A_WIDTH), BF16),
                   jax.ShapeDtypeStruct((ROW_BLK, A_WIDTH), F32)],
        compiler_params=_params(1),
        name="mixer_a",
    )(h3, h3, w_stack, bt_stack, ln_g.reshape(1, -1), ln_b.reshape(1, -1))


def _attention_heads(q, k, v, neg_dist, mask_bias, sinks_ref, o_ref, row0):
    grp = N_HEADS // N_KV_HEADS
    m_rows = q.shape[0]
    for kv in range(N_KV_HEADS):
        kg = k[:, kv * HEAD_DIM:(kv + 1) * HEAD_DIM]
        vg = v[:, kv * HEAD_DIM:(kv + 1) * HEAD_DIM]
        outs = []
        for j in range(grp):
            h = kv * grp + j
            qh = q[:, h * HEAD_DIM:(h + 1) * HEAD_DIM]
            s = lax.dot_general(qh, kg, (((1,), (1,)), ((), ())), preferred_element_type=F32)
            s = s + (ALIBI_SLOPES[h] * neg_dist + mask_bias)
            sink = sinks_ref[h]
            m = jnp.maximum(jnp.max(s, axis=-1, keepdims=True), sink)
            p = jnp.exp(s - m)
            den = jnp.sum(p, axis=-1, keepdims=True) + jnp.exp(sink - m)
            outs.append(jnp.dot(p.astype(BF16), vg, preferred_element_type=F32) / den)
        for j in range(0, grp, 2):
            c0 = (kv * grp + j) * HEAD_DIM
            o_ref[row0:row0 + m_rows, c0:c0 + 2 * HEAD_DIM] = jnp.concatenate(outs[j:j + 2], axis=1).astype(BF16)


def _attn_prompt_kernel(sinks_ref, q_ref, kv_ref, halo_ref, mix_ref, o_ref):
    del mix_ref
    i = pl.program_id(0)
    kv_all = jnp.concatenate([halo_ref[...], kv_ref[...]], axis=0).astype(BF16)
    half = WINDOW
    n_keys = 2 * WINDOW
    r = lax.broadcasted_iota(I32, (half, n_keys), 0)
    c = lax.broadcasted_iota(I32, (half, n_keys), 1)
    neg_dist = -jnp.abs(r + WINDOW - c).astype(F32)
    qc = r // CHUNK
    kc = c // CHUNK
    band = (kc >= qc) & (kc <= qc + WINDOW // CHUNK)
    for hf in range(ROW_BLK // half):
        visible = band & (c >= jnp.where(i > 0, 0, WINDOW)) if hf == 0 else band
        keys = kv_all[hf * half:hf * half + n_keys]
        q = (q_ref[hf * half:(hf + 1) * half, :] * (HEAD_DIM ** -0.5)).astype(BF16)
        _attention_heads(q, keys[:, :KV_WIDTH], keys[:, KV_WIDTH:], neg_dist,
                         jnp.where(visible, 0.0, NEG_INF), sinks_ref, o_ref, hf * half)


def _attn_prompt(h3, kv, mix, sinks, n_prompt_blk):
    t = h3.shape[0]
    halo_per_blk = ROW_BLK // WINDOW
    return pl.pallas_call(
        _attn_prompt_kernel,
        grid=(n_prompt_blk,),
        in_specs=[pl.BlockSpec(memory_space=pltpu.SMEM),
                  pl.BlockSpec((ROW_BLK, Q_WIDTH), lambda i: (i, 2)),
                  pl.BlockSpec((ROW_BLK, 2 * KV_WIDTH), lambda i: (i, 0)),
                  pl.BlockSpec((WINDOW, 2 * KV_WIDTH), lambda i: (jnp.maximum(i * halo_per_blk - 1, 0), 0)),
                  pl.BlockSpec(memory_space=pl.ANY)],
        out_specs=pl.BlockSpec((ROW_BLK, Q_WIDTH), lambda i: (i, 1)),
        out_shape=jax.ShapeDtypeStruct((t, 2 * Q_WIDTH), BF16),
        input_output_aliases={4: 0},
        compiler_params=_params(1),
        name="attn_prompt",
    )(sinks, h3, kv, kv, mix)


def _attn_sample_kernel(n_streams, seq_s, sinks_ref, q_ref, kv_ref, ck_ref, cv_ref, mix_ref, o_ref):
    del mix_ref
    n_cache = ck_ref.shape[0]
    w_c = n_cache // n_streams
    k_all = jnp.concatenate([ck_ref[...], kv_ref[:, :KV_WIDTH]], axis=0).astype(BF16)
    v_all = jnp.concatenate([cv_ref[...], kv_ref[:, KV_WIDTH:]], axis=0).astype(BF16)
    n_keys = n_cache + ROW_BLK
    r = lax.broadcasted_iota(I32, (ROW_BLK, n_keys), 0)
    c = lax.broadcasted_iota(I32, (ROW_BLK, n_keys), 1)
    is_new = c >= n_cache
    k_stream = jnp.where(is_new, (c - n_cache) // seq_s, c // w_c)
    k_pos = jnp.where(is_new, w_c + (c - n_cache) % seq_s, c % w_c)
    neg_dist = -jnp.abs(w_c + r % seq_s - k_pos).astype(F32)
    mask_bias = jnp.where(k_stream == r // seq_s, 0.0, NEG_INF)
    q = (q_ref[...] * (HEAD_DIM ** -0.5)).astype(BF16)
    _attention_heads(q, k_all, v_all, neg_dist, mask_bias, sinks_ref, o_ref, 0)


def _attn_sample(h3, kv, mix, cache_k2d, cache_v2d, sinks, n_prompt_blk, n_streams, seq_s):
    t = h3.shape[0]
    full = lambda i: (0, 0)
    return pl.pallas_call(
        functools.partial(_attn_sample_kernel, n_streams, seq_s),
        grid=(1,),
        in_specs=[pl.BlockSpec(memory_space=pltpu.SMEM),
                  pl.BlockSpec((ROW_BLK, Q_WIDTH), lambda i: (n_prompt_blk, 2)),
                  pl.BlockSpec((ROW_BLK, 2 * KV_WIDTH), lambda i: (n_prompt_blk, 0)),
                  pl.BlockSpec(cache_k2d.shape, full),
                  pl.BlockSpec(cache_v2d.shape, full),
                  pl.BlockSpec(memory_space=pl.ANY)],
        out_specs=pl.BlockSpec((ROW_BLK, Q_WIDTH), lambda i: (n_prompt_blk, 1)),
        out_shape=jax.ShapeDtypeStruct((t, 2 * Q_WIDTH), BF16),
        input_output_aliases={5: 0},
        compiler_params=_params(1),
        name="attn_sample",
    )(sinks, h3, kv, cache_k2d, cache_v2d, mix)


def _conv_kernel(n_prompt_blk, n_streams, seq_s, z_ref, halo_ref, go_ref, st_ref, cw_ref, o_ref):
    i = pl.program_id(0)
    z = z_ref[...]
    rows = lax.broadcasted_iota(I32, (ROW_BLK, 1), 0)
    zm1 = pltpu.roll(z, 1, 0)
    zm2 = pltpu.roll(z, 2, 0)

    def finish(zm1, zm2):
        conv = cw_ref[0:1, :] * zm2 + cw_ref[1:2, :] * zm1 + cw_ref[2:3, :] * z
        o_ref[...] = (go_ref[...] * conv).astype(BF16)

    @pl.when(i < n_prompt_blk)
    def _():
        live = jnp.where(i > 0, 1.0, 0.0)
        p1 = halo_ref[7:8, :] * live
        p2 = halo_ref[6:7, :] * live
        finish(jnp.where(rows == 0, p1, zm1),
               jnp.where(rows == 0, p2, jnp.where(rows == 1, p1, zm2)))

    @pl.when(i == n_prompt_blk)
    def _():
        a, b = zm1, zm2
        for s in range(n_streams):
            s0 = st_ref[2 * s:2 * s + 1, :]
            s1 = st_ref[2 * s + 1:2 * s + 2, :]
            a = jnp.where(rows == s * seq_s, s1, a)
            b = jnp.where(rows == s * seq_s, s0, jnp.where(rows == s * seq_s + 1, s1, b))
        finish(a, b)


def _gated_conv(z, gate_out, state2d, conv_w, n_prompt_blk, n_streams, seq_s):
    t = z.shape[0]
    halo_per_blk = ROW_BLK // 8
    return pl.pallas_call(
        functools.partial(_conv_kernel, n_prompt_blk, n_streams, seq_s),
        grid=(t // ROW_BLK,),
        in_specs=[pl.BlockSpec((ROW_BLK, D_MODEL), lambda i: (i, 0)),
                  pl.BlockSpec((8, D_MODEL), lambda i: (jnp.maximum(i * halo_per_blk - 1, 0), 0)),
                  pl.BlockSpec((ROW_BLK, D_MODEL), lambda i: (i, 0)),
                  pl.BlockSpec(state2d.shape, lambda i: (0, 0)),
                  pl.BlockSpec(conv_w.shape, lambda i: (0, 0))],
        out_specs=pl.BlockSpec((ROW_BLK, D_MODEL), lambda i: (i, 0)),
        out_shape=jax.ShapeDtypeStruct((t, D_MODEL), BF16),
        compiler_params=_params(1),
        name="gated_conv",
    )(z, z, gate_out, state2d, conv_w)


def _split_bf16(x):
    hi = x.astype(BF16)
    lo = (x - hi.astype(F32)).astype(BF16)
    return hi, lo


def _router_kernel(n_blk, x_ref, w_ref, br_ref, ri_ref, gate_ref, cnt_ref, carry_ref, wsplit_ref):
    i = pl.program_id(0)

    @pl.when(i == 0)
    def _():
        carry_ref[...] = jnp.zeros_like(carry_ref)
        wh, wl = _split_bf16(w_ref[...])
        wsplit_ref[0] = wh
        wsplit_ref[1] = wl

    xh, xl = _split_bf16(x_ref[...])
    wh, wl = wsplit_ref[0], wsplit_ref[1]
    logits = (jnp.dot(xh, wh, preferred_element_type=F32) + jnp.dot(xh, wl, preferred_element_type=F32)
              + jnp.dot(xl, wh, preferred_element_type=F32))
    lt = logits.T[:N_EXPERTS]
    ex = jnp.exp(lt - jnp.max(lt, axis=0, keepdims=True))
    scores = ex / jnp.sum(ex, axis=0, keepdims=True)
    biased = scores + br_ref[...]
    sc = [scores[e:e + 1] for e in range(N_EXPERTS)]
    bs = [biased[e:e + 1] for e in range(N_EXPERTS)]

    gscore = []
    for g in range(N_GROUPS):
        a, b, c, d = bs[4 * g:4 * g + 4]
        hi1, lo1 = jnp.maximum(a, b), jnp.minimum(a, b)
        hi2, lo2 = jnp.maximum(c, d), jnp.minimum(c, d)
        gscore.append(jnp.maximum(hi1, hi2) + jnp.maximum(jnp.minimum(hi1, hi2), jnp.maximum(lo1, lo2)))
    best, grp = gscore[0], jnp.zeros((1, ROW_BLK), I32)
    for g in range(1, N_GROUPS):
        better = gscore[g] > best
        grp = jnp.where(better, g, grp)
        best = jnp.where(better, gscore[g], best)

    def pick(rows, idx, n):
        out = rows[0]
        for j in range(1, n):
            out = jnp.where(idx == j, rows[j], out)
        return out

    in_b = [pick([bs[4 * g + j] for g in range(N_GROUPS)], grp, N_GROUPS) for j in range(EXPERTS_PER_GROUP)]
    in_s = [pick([sc[4 * g + j] for g in range(N_GROUPS)], grp, N_GROUPS) for j in range(EXPERTS_PER_GROUP)]
    v1, l1 = in_b[0], jnp.zeros((1, ROW_BLK), I32)
    for j in range(1, EXPERTS_PER_GROUP):
        better = in_b[j] > v1
        l1 = jnp.where(better, j, l1)
        v1 = jnp.where(better, in_b[j], v1)
    v2, l2 = jnp.full((1, ROW_BLK), -jnp.inf, F32), jnp.zeros((1, ROW_BLK), I32)
    for j in range(EXPERTS_PER_GROUP):
        better = (l1 != j) & (in_b[j] > v2)
        l2 = jnp.where(better, j, l2)
        v2 = jnp.where(better, in_b[j], v2)
    gate1 = pick(in_s, l1, EXPERTS_PER_GROUP)
    gate2 = pick(in_s, l2, EXPERTS_PER_GROUP)
    gsum = gate1 + gate2
    gate_ref[0:1, :] = gate1 / gsum
    gate_ref[1:2, :] = gate2 / gsum
    e1 = grp * EXPERTS_PER_GROUP + l1
    e2 = grp * EXPERTS_PER_GROUP + l2
    ri_ref[0:1, :] = e1
    ri_ref[1:2, :] = e2

    expert = lax.broadcasted_iota(I32, (N_EXPERTS, ROW_BLK), 0)
    oh1 = (expert == e1).astype(F32)
    oh2 = (expert == e2).astype(F32)
    r = lax.broadcasted_iota(I32, (ROW_BLK, ROW_BLK), 0)
    c = lax.broadcasted_iota(I32, (ROW_BLK, ROW_BLK), 1)
    before = (r < c).astype(BF16)
    cs = jnp.dot(jnp.concatenate([oh1, oh2], axis=0).astype(BF16), before, preferred_element_type=F32)
    tot1 = jnp.sum(oh1, axis=1, keepdims=True)
    tot2 = jnp.sum(oh2, axis=1, keepdims=True)
    carry = carry_ref[...][:, 0:1]
    ri_ref[2:3, :] = jnp.sum(oh1 * (carry + cs[:N_EXPERTS]), axis=0, keepdims=True).astype(I32)
    ri_ref[3:4, :] = jnp.sum(oh2 * (carry + tot1 + cs[N_EXPERTS:]), axis=0, keepdims=True).astype(I32)
    carry_ref[...] = carry_ref[...] + tot1 + tot2

    @pl.when(i == n_blk - 1)
    def _():
        cnt_ref[...] = carry_ref[...]


def _router(x, w_pad, b_col):
    t = x.shape[0]
    n_blk = t // ROW_BLK
    return pl.pallas_call(
        functools.partial(_router_kernel, n_blk),
        grid=(n_blk,),
        in_specs=[pl.BlockSpec((ROW_BLK, D_MODEL), lambda i: (i, 0)),
                  pl.BlockSpec((D_MODEL, LANES), lambda i: (0, 0)),
                  pl.BlockSpec((N_EXPERTS, 1), lambda i: (0, 0))],
        out_specs=[pl.BlockSpec((4, ROW_BLK), lambda i: (0, i)),
                   pl.BlockSpec((2, ROW_BLK), lambda i: (0, i)),
                   pl.BlockSpec((N_EXPERTS, LANES), lambda i: (0, 0))],
        out_shape=[jax.ShapeDtypeStruct((4, t), I32),
                   jax.ShapeDtypeStruct((2, t), F32),
                   jax.ShapeDtypeStruct((N_EXPERTS, LANES), F32)],
        scratch_shapes=[pltpu.VMEM((N_EXPERTS, LANES), F32), pltpu.VMEM((2, D_MODEL, LANES), BF16)],
        compiler_params=_params(1),
        name="router",
    )(x, w_pad, b_col)


def _dispatch_kernel(pos_ref, meta_ref, x_ref, xs_ref, sem):
    i = pl.program_id(0)
    base = i * ROW_BLK

    def row_copy(src_row, dst_row):
        return pltpu.make_async_copy(x_ref.at[pl.ds(src_row, 1)], xs_ref.at[pl.ds(dst_row, 1)], sem)

    def group(g, carry):
        tile_rows = x_ref.at[pl.ds(pl.multiple_of(g * SUBLANES, SUBLANES), SUBLANES)]
        for j in range(SUBLANES):
            t = base + g * SUBLANES + j
            for k in range(2):
                pltpu.make_async_copy(tile_rows.at[pl.ds(j, 1)], xs_ref.at[pl.ds(pos_ref[2 * t + k], 1)],
                                      sem).start(priority=k)
        return carry

    lax.fori_loop(0, ROW_BLK // SUBLANES, group, 0)

    @pl.when(i == 0)
    def _():
        for e in range(N_EXPERTS):
            cnt, start, padded = meta_ref[e], meta_ref[N_EXPERTS + e], meta_ref[2 * N_EXPERTS + e]

            def fill(ri, carry):
                row_copy(0, start + ri).start()
                return carry

            def drain(ri, carry):
                row_copy(0, 0).wait()
                return carry

            lax.fori_loop(cnt, padded, fill, 0)
            lax.fori_loop(cnt, padded, drain, 0)

    for _ in range(2):
        pltpu.make_async_copy(x_ref, xs_ref.at[pl.ds(0, ROW_BLK)], sem).wait()


def _dispatch(pos_flat, meta, x, n_rows):
    t = x.shape[0]
    grid_spec = pltpu.PrefetchScalarGridSpec(
        num_scalar_prefetch=2,
        grid=(t // ROW_BLK,),
        in_specs=[pl.BlockSpec((ROW_BLK, D_MODEL), lambda i, pos, meta: (i, 0))],
        out_specs=pl.BlockSpec(memory_space=pl.ANY),
        scratch_shapes=[pltpu.SemaphoreType.DMA(())],
    )
    return pl.pallas_call(
        _dispatch_kernel,
        grid_spec=grid_spec,
        out_shape=jax.ShapeDtypeStruct((n_rows, D_MODEL), x.dtype),
        compiler_params=_params(1),
        name="dispatch",
    )(pos_flat, meta, x)


def _expert_mlp_kernel(li, first_ref, nt_ref, row0_ref, slot_ref, next_ref, xs_ref, wg_ref, wu_ref, wd_ref,
                       ys_ref, wgb, wub, wdb, sg, su, sd, xstage, ostage, w_sem, x_sem, y_sem):
    e = pl.program_id(0)
    nt = nt_ref[e]
    row0 = row0_ref[e]
    slot = slot_ref[e]
    e_next = next_ref[e]
    prefetch = e_next >= 0
    w_parts = ((wg_ref, sg, wgb, D_MODEL // W_CHUNKS),
               (wu_ref, su, wub, D_MODEL // W_CHUNKS),
               (wd_ref, sd, wdb, D_EXPERT // W_CHUNKS))

    def w_copy(k, expert, c):
        src, stage, _, rows = w_parts[k]
        s = c % W_STAGES
        return pltpu.make_async_copy(src.at[li, expert, pl.ds(pl.multiple_of(c * rows, rows), rows)],
                                     stage.at[s], w_sem.at[k, s])

    def w_start(expert):
        for c in range(W_STAGES - 1):
            for k in range(3):
                w_copy(k, expert, c).start()

    def w_chunk(expert, c, dst_slot):
        for k in range(3):
            w_copy(k, expert, c).wait()

        @pl.when(c + W_STAGES - 1 < W_CHUNKS)
        def _():
            for k in range(3):
                w_copy(k, expert, c + W_STAGES - 1).start()

        for _, stage, dst, rows in w_parts:
            dst[dst_slot, pl.ds(pl.multiple_of(c * rows, rows), rows), :] = stage[c % W_STAGES].astype(BF16)

    def hbm_rows(r):
        return pl.ds(pl.multiple_of(row0 + r * MOE_ROWS, MOE_ROWS), MOE_ROWS)

    def x_load(r, s):
        return pltpu.make_async_copy(xs_ref.at[hbm_rows(r)], xstage.at[s], x_sem.at[s])

    def y_store(r, s):
        return pltpu.make_async_copy(ostage.at[s], ys_ref.at[hbm_rows(r)], y_sem.at[s])

    def tile(r):
        s = r % 2
        x_load(r, s).wait()

        @pl.when(r + 1 < nt)
        def _():
            x_load(r + 1, 1 - s).start()

        xb = xstage[s].astype(BF16)
        g = jnp.dot(xb, wgb[slot], preferred_element_type=F32)
        u = jnp.dot(xb, wub[slot], preferred_element_type=F32)
        h = (g * jax.nn.sigmoid(g) * u).astype(BF16)
        out = jnp.dot(h, wdb[slot], preferred_element_type=F32)

        @pl.when(r >= 2)
        def _():
            y_store(r - 2, s).wait()

        ostage[s] = out
        y_store(r, s).start()

    @pl.when(e == first_ref[0])
    def _():
        w_start(e)

        def own(c, carry):
            w_chunk(e, c, slot)
            return carry

        lax.fori_loop(0, W_CHUNKS, own, 0)

    @pl.when(nt > 0)
    def _():
        @pl.when(prefetch)
        def _():
            w_start(e_next)

        x_load(0, 0).start()

        def step(r, carry):
            @pl.when(r < nt)
            def _():
                tile(r)

            @pl.when(prefetch & (r < W_CHUNKS))
            def _():
                w_chunk(e_next, r, 1 - slot)

            return carry

        lax.fori_loop(0, jnp.maximum(nt, jnp.where(prefetch, W_CHUNKS, 0)), step, 0)
        y_store(0, (nt - 1) % 2).wait()

        @pl.when(nt >= 2)
        def _():
            y_store(0, nt % 2).wait()


def _expert_mlp(first, tiles_e, starts, slot_e, next_e, xs, w_gate, w_up, w_down, li):
    n_rows = xs.shape[0]
    qg, qd = D_MODEL // W_CHUNKS, D_EXPERT // W_CHUNKS
    grid_spec = pltpu.PrefetchScalarGridSpec(
        num_scalar_prefetch=5,
        grid=(N_EXPERTS,),
        in_specs=[pl.BlockSpec(memory_space=pl.ANY)] * 4,
        out_specs=pl.BlockSpec(memory_space=pl.ANY),
        scratch_shapes=[pltpu.VMEM((2, D_MODEL, D_EXPERT), BF16),
                        pltpu.VMEM((2, D_MODEL, D_EXPERT), BF16),
                        pltpu.VMEM((2, D_EXPERT, D_MODEL), BF16),
                        pltpu.VMEM((W_STAGES, qg, D_EXPERT), F32),
                        pltpu.VMEM((W_STAGES, qg, D_EXPERT), F32),
                        pltpu.VMEM((W_STAGES, qd, D_MODEL), F32),
                        pltpu.VMEM((2, MOE_ROWS, D_MODEL), xs.dtype),
                        pltpu.VMEM((2, MOE_ROWS, D_MODEL), F32),
                        pltpu.SemaphoreType.DMA((3, W_STAGES)),
                        pltpu.SemaphoreType.DMA((2,)),
                        pltpu.SemaphoreType.DMA((2,))],
    )
    return pl.pallas_call(
        functools.partial(_expert_mlp_kernel, li),
        grid_spec=grid_spec,
        out_shape=jax.ShapeDtypeStruct((n_rows, D_MODEL), F32),
        compiler_params=_params(1),
        name="expert_mlp",
    )(first, tiles_e, starts, slot_e, next_e, xs, w_gate, w_up, w_down)


def _combine_kernel(n_prompt_blk, pos_ref, ys_ref, x_ref, gate_ref, g_ref, b_ref, oa_ref, ob_ref, buf_ref, sem):
    i = pl.program_id(0)
    slot = i % 2

    def gather(blk, s):
        def group(g, carry):
            for k in range(2):
                tile_rows = buf_ref.at[s, k, pl.ds(pl.multiple_of(g * SUBLANES, SUBLANES), SUBLANES)]
                for j in range(SUBLANES):
                    t = blk * ROW_BLK + g * SUBLANES + j
                    pltpu.make_async_copy(ys_ref.at[pl.ds(pos_ref[2 * t + k], 1)],
                                          tile_rows.at[pl.ds(j, 1)], sem.at[s]).start(priority=k)
            return carry

        lax.fori_loop(0, ROW_BLK // SUBLANES, group, 0)

    @pl.when(i == 0)
    def _():
        gather(0, 0)

    @pl.when(i + 1 < pl.num_programs(0))
    def _():
        gather(i + 1, 1 - slot)

    for k in range(2):
        pltpu.make_async_copy(ys_ref.at[pl.ds(0, ROW_BLK)], buf_ref.at[slot, k], sem.at[slot]).wait()
    gates = gate_ref[...]
    y = ALPHA * x_ref[...] + gates[:, 0:1] * buf_ref[slot, 0] + gates[:, 1:2] * buf_ref[slot, 1]
    y = _layer_norm(y, g_ref[...], b_ref[...])
    if n_prompt_blk is None:
        oa_ref[...] = y
        ob_ref[...] = y.astype(BF16)
    else:
        @pl.when(i < n_prompt_blk)
        def _():
            oa_ref[...] = y

        @pl.when(i == n_prompt_blk)
        def _():
            ob_ref[...] = y


def _combine_ln(pos_flat, ys, x, gates, g, b, n_prompt_blk=None):
    t = x.shape[0]
    row = lambda i, pos: (i, 0)
    vec = lambda i, pos: (0, 0)
    if n_prompt_blk is None:
        out_specs = [pl.BlockSpec((ROW_BLK, D_MODEL), row), pl.BlockSpec((ROW_BLK, D_MODEL), row)]
        out_shape = [jax.ShapeDtypeStruct((t, D_MODEL), F32), jax.ShapeDtypeStruct((t, D_MODEL), BF16)]
    else:
        out_specs = [pl.BlockSpec((ROW_BLK, D_MODEL), lambda i, pos: (jnp.minimum(i, n_prompt_blk - 1), 0)),
                     pl.BlockSpec((ROW_BLK, D_MODEL), vec)]
        out_shape = [jax.ShapeDtypeStruct((n_prompt_blk * ROW_BLK, D_MODEL), F32),
                     jax.ShapeDtypeStruct((t - n_prompt_blk * ROW_BLK, D_MODEL), F32)]
    grid_spec = pltpu.PrefetchScalarGridSpec(
        num_scalar_prefetch=1,
        grid=(t // ROW_BLK,),
        in_specs=[pl.BlockSpec(memory_space=pl.ANY),
                  pl.BlockSpec((ROW_BLK, D_MODEL), row),
                  pl.BlockSpec((ROW_BLK, 2), row),
                  pl.BlockSpec((1, D_MODEL), vec),
                  pl.BlockSpec((1, D_MODEL), vec)],
        out_specs=out_specs,
        scratch_shapes=[pltpu.VMEM((2, 2, ROW_BLK, D_MODEL), F32), pltpu.SemaphoreType.DMA((2,))],
    )
    return pl.pallas_call(
        functools.partial(_combine_kernel, n_prompt_blk),
        grid_spec=grid_spec,
        out_shape=out_shape,
        compiler_params=_params(1),
        name="combine_ln",
    )(pos_flat, ys, x, gates, g.reshape(1, -1), b.reshape(1, -1))


def _moe_layer(x, w_pad, br_pad, w_gate, w_up, w_down, li, ln_g, ln_b, n_prompt_blk=None):
    t = x.shape[0]
    n_tiles = -(-(2 * t + N_EXPERTS * (MOE_ROWS - 1)) // MOE_ROWS)
    ri, gates, cnt = _router(x, w_pad, br_pad)
    gates = gates.T
    counts = cnt[:, 0].astype(I32)
    tiles_e = (counts + MOE_ROWS - 1) // MOE_ROWS
    padded = tiles_e * MOE_ROWS
    starts = (jnp.cumsum(padded) - padded).astype(I32)
    onehot = ri[0:2, :, None] == jnp.arange(N_EXPERTS, dtype=I32)
    pos = jnp.sum(jnp.where(onehot, starts, 0), axis=-1) + ri[2:4]
    pos_flat = pos.T.reshape(-1)
    meta = jnp.concatenate([counts, starts, padded]).astype(I32)
    ids = jnp.arange(N_EXPERTS, dtype=I32)
    live = tiles_e > 0
    slot_e = ((jnp.cumsum(live) - live) % 2).astype(I32)
    later = live[None, :] & (ids[None, :] > ids[:, None])
    next_e = jnp.min(jnp.where(later, ids[None, :], N_EXPERTS), axis=1)
    next_e = jnp.where(next_e < N_EXPERTS, next_e, -1).astype(I32)
    first = jnp.min(jnp.where(live, ids, N_EXPERTS - 1)).astype(I32).reshape(1)

    xs = _dispatch(pos_flat, meta, x, n_tiles * MOE_ROWS)
    ys = _expert_mlp(first, tiles_e.astype(I32), starts, slot_e, next_e, xs, w_gate, w_up, w_down, li)
    return _combine_ln(pos_flat, ys, x, gates, ln_g, ln_b, n_prompt_blk)


def kernel(x_prompt, x_sample, cache_k, cache_v, state_conv, w_in_even, ln_v_g, ln_v_b, w_spatial, b_spatial, sinks, w_out_even, w_in_odd, conv_w, w_out_odd, ln_mix_g, ln_mix_b, ln_ffn_g, ln_ffn_b, w_router, b_router, w_gate, w_up, w_down):
    n_batch, seq_p, _ = x_prompt.shape
    n_streams, seq_s, _ = x_sample.shape
    n_p = n_batch * seq_p
    n_s = n_streams * seq_s
    assert n_batch == 1 and n_s == ROW_BLK and n_p % ROW_BLK == 0 and (n_p + n_s) % MM_ROWS == 0
    assert seq_s & (seq_s - 1) == 0 and A_CHUNK % seq_s == 0 and cache_k.shape[2] == WINDOW
    n_prompt_blk = n_p // ROW_BLK
    win_p = min(WINDOW, seq_p)

    x = jnp.concatenate([x_prompt.reshape(n_p, D_MODEL), x_sample.reshape(n_s, D_MODEL)], axis=0)
    xb = x
    w_pad = jnp.pad(w_router, ((0, 0), (0, LANES - N_EXPERTS)))
    br_pad = b_router.reshape(N_EXPERTS, 1)
    reps = A_CHUNK // seq_s

    k_p, v_p, k_s, v_s, vg_s, c_p, c_s = [], [], [], [], [], [], []
    for layer in range(DEPTH):
        i = layer // 2
        if layer % 2 == 0:
            h3, kv = _even_in_proj(xb, w_in_even, i)
            w_stack = jnp.stack([w_spatial[i], jnp.tile(w_spatial[i][:, :seq_s, :seq_s], (1, reps, reps))])
            bt_stack = jnp.stack([b_spatial[i].T, jnp.tile(b_spatial[i][:, :seq_s], (1, reps)).T])
            mix, vn_s = _mixer_a(h3, w_stack, bt_stack, ln_v_g[i], ln_v_b[i], n_prompt_blk, seq_s)
            mix = _attn_prompt(h3, kv, mix, sinks[i], n_prompt_blk)
            mix = _attn_sample(h3, kv, mix, cache_k[i].reshape(n_streams * WINDOW, KV_WIDTH),
                               cache_v[i].reshape(n_streams * WINDOW, KV_WIDTH), sinks[i],
                               n_prompt_blk, n_streams, seq_s)
            x, xb = _out_proj_ln(mix, w_out_even, i, x, ln_mix_g[layer], ln_mix_b[layer])
            k_p.append(kv[n_p - win_p:n_p, :KV_WIDTH].reshape(n_batch, win_p, N_KV_HEADS, HEAD_DIM))
            v_p.append(kv[n_p - win_p:n_p, KV_WIDTH:].reshape(n_batch, win_p, N_KV_HEADS, HEAD_DIM))
            k_s.append(kv[n_p:, :KV_WIDTH].reshape(n_streams, seq_s, N_KV_HEADS, HEAD_DIM))
            v_s.append(kv[n_p:, KV_WIDTH:].reshape(n_streams, seq_s, N_KV_HEADS, HEAD_DIM))
            vg_s.append(vn_s.reshape(n_streams, seq_s, A_WIDTH))
        else:
            gate_out, z = _odd_in_proj(xb, w_in_odd, i)
            g = _gated_conv(z, gate_out, state_conv[i].reshape(n_streams * (CONV_WIDTH - 1), D_MODEL),
                            conv_w[i], n_prompt_blk, n_streams, seq_s)
            x, xb = _out_proj_ln(g, w_out_odd, i, x, ln_mix_g[layer], ln_mix_b[layer])
            c_p.append(z[n_p - (CONV_WIDTH - 1):n_p].reshape(n_batch, CONV_WIDTH - 1, D_MODEL))
            c_s.append(z[n_p:].reshape(n_streams, seq_s, D_MODEL)[:, seq_s - (CONV_WIDTH - 1):])
        x, xb = _moe_layer(x, w_pad, br_pad, w_gate, w_up, w_down, layer,
                           ln_ffn_g[layer], ln_ffn_b[layer],
                           n_prompt_blk if layer == DEPTH - 1 else None)
    return (x.reshape(n_batch, seq_p, D_MODEL), xb.reshape(n_streams, seq_s, D_MODEL),
            jnp.stack(k_p), jnp.stack(v_p), jnp.stack(k_s), jnp.stack(v_s),
            jnp.stack(vg_s), jnp.stack(c_p), jnp.stack(c_s))
```
